```python
import jax, jax.numpy as jnp
from jax import lax
import numpy as np

D_MODEL = 1024
BATCH = 8
SEQ = 4096
DEPTH = 2
DEC_BATCH = 16
DEC_SEQ = 4096
PAST_LEN = 128

A_HEADS = 16
A_KV_HEADS = 4
A_HEAD_DIM = D_MODEL // A_HEADS
A_WIDTH = A_HEADS * A_HEAD_DIM
A_KV_WIDTH = A_KV_HEADS * A_HEAD_DIM
WINDOW = 128
BLOCK = 128
A_SPLITS = (A_WIDTH, A_KV_WIDTH, A_KV_WIDTH, A_WIDTH)

G_HEADS = 4
G_KEY_WIDTH = D_MODEL // 2
G_VAL_WIDTH = D_MODEL
G_KEY_DIM = G_KEY_WIDTH // G_HEADS
G_VAL_DIM = G_VAL_WIDTH // G_HEADS
G_GATE_RANK = 16
G_GATE_TAU = 16.0
G_CHUNK = 64
G_SPLITS = (G_KEY_WIDTH, G_KEY_WIDTH, G_VAL_WIDTH, G_VAL_WIDTH, G_GATE_RANK, G_GATE_RANK)

LN_EPS = 1e-5
RMS_EPS = 1e-6
DN_ALPHA = (2 * DEPTH) ** 0.25
DN_BETA = (8 * DEPTH) ** -0.25

kernel_name = "hybrid_swa_gla_deepnorm_encoder"


def _split(h, sizes):
    idx = [int(i) for i in np.cumsum(sizes)[:-1]]
    return jnp.split(h, idx, axis=-1)


def layer_norm(x, g, b):
    xf = x.astype(jnp.float32)
    mu = jnp.mean(xf, axis=-1, keepdims=True)
    var = jnp.mean(jnp.square(xf - mu), axis=-1, keepdims=True)
    y = (xf - mu) * lax.rsqrt(var + LN_EPS) * g.astype(jnp.float32) + b.astype(jnp.float32)
    return y.astype(x.dtype)


def alibi_slopes(n_heads):
    return jnp.asarray(2.0 ** (-8.0 * np.arange(1, n_heads + 1) / n_heads), dtype=jnp.float32)


def window_attention(q, k, v, sink):
    B, S, H, dh = q.shape
    KV = k.shape[2]
    rep = H // KV
    nb = S // BLOCK
    qb = (q * dh ** -0.5).reshape(B, nb, BLOCK, KV, rep, dh)
    pad = ((0, 0), (BLOCK, BLOCK), (0, 0), (0, 0))
    kp = jnp.pad(k, pad).reshape(B, nb + 2, BLOCK, KV, dh)
    vp = jnp.pad(v, pad).reshape(B, nb + 2, BLOCK, KV, dh)
    kw = jnp.concatenate([kp[:, :-2], kp[:, 1:-1], kp[:, 2:]], axis=2)
    vw = jnp.concatenate([vp[:, :-2], vp[:, 1:-1], vp[:, 2:]], axis=2)
    qi = jnp.arange(BLOCK)[:, None] + BLOCK
    kj = jnp.arange(3 * BLOCK)[None, :]
    dist = jnp.abs(qi - kj).astype(jnp.float32)
    in_win = dist <= WINDOW
    kpos = (jnp.arange(nb)[:, None] - 1) * BLOCK + jnp.arange(3 * BLOCK)[None, :]
    kvalid = (kpos >= 0) & (kpos < S)
    slopes = alibi_slopes(H).reshape(KV, rep)
    bias = -slopes[:, :, None, None] * dist
    sink_l = sink.astype(jnp.float32).reshape(KV, rep)

    def one_block(args):
        qn, kn, vn, valid = args
        s = jnp.einsum('bqgrd,bkgd->bgrqk', qn, kn).astype(jnp.float32) + bias
        s = jnp.where(in_win & valid[None, :], s, -1e30)
        sk = jnp.broadcast_to(sink_l[None, :, :, None, None], s.shape[:-1] + (1,))
        p = jax.nn.softmax(jnp.concatenate([s, sk], axis=-1), axis=-1)[..., :-1]
        return jnp.einsum('bgrqk,bkgd->bqgrd', p.astype(vn.dtype), vn)

    out = lax.map(one_block, (jnp.moveaxis(qb, 1, 0), jnp.moveaxis(kw, 1, 0),
                              jnp.moveaxis(vw, 1, 0), kvalid))
    return jnp.moveaxis(out, 0, 1).reshape(B, S, H * dh)


def gla_direction(q, k, v, log_a, inclusive):
    B, S, H, dk = q.shape
    dv = v.shape[-1]
    nc = S // G_CHUNK

    def chunks(t):
        return jnp.moveaxis(t.reshape(B, nc, G_CHUNK, H, t.shape[-1]), 1, 0)

    qc, kc, vc = chunks(q), chunks(k), chunks(v)
    cum = jnp.cumsum(chunks(log_a.astype(jnp.float32)), axis=2)
    idx = jnp.arange(G_CHUNK)
    mask = (idx[:, None] >= idx[None, :]) if inclusive else (idx[:, None] > idx[None, :])

    def step(state, xs):
        qn, kn, vn, bn = xs
        qf, kf, vf = qn.astype(jnp.float32), kn.astype(jnp.float32), vn.astype(jnp.float32)
        b_last = bn[:, -1]
        q_dec = qf * jnp.exp(bn)
        k_inv = kf * jnp.exp(-bn)
        k_tail = kf * jnp.exp(b_last[:, None] - bn)
        inter = jnp.einsum('bthk,bhkv->bthv', q_dec, state)
        att = jnp.where(mask, jnp.einsum('bthk,bshk->bhts', q_dec, k_inv), 0.0)
        intra = jnp.einsum('bhts,bshv->bthv', att, vf)
        new_state = jnp.exp(b_last)[..., None] * state + jnp.einsum('bshk,bshv->bhkv', k_tail, vf)
        return new_state, inter + intra

    state0 = jnp.zeros((B, H, dk, dv), jnp.float32)
    _, out = lax.scan(step, state0, (qc, kc, vc, cum))
    return jnp.moveaxis(out, 0, 1).reshape(B, S, H, dv)


def attn_mixer(x, w_in, sink, w_out):
    B, S, _ = x.shape
    q, k, v, gate = _split(x @ w_in, A_SPLITS)
    o = window_attention(q.reshape(B, S, A_HEADS, A_HEAD_DIM),
                         k.reshape(B, S, A_KV_HEADS, A_HEAD_DIM),
                         v.reshape(B, S, A_KV_HEADS, A_HEAD_DIM), sink)
    return (o * jax.nn.silu(gate)) @ w_out


def gla_mixer(x, w_in, w_gate_f, b_gate_f, w_gate_b, b_gate_b, head_norm, w_out):
    B, S, _ = x.shape
    q, k, v, gate, lr_f, lr_b = _split(x @ w_in, G_SPLITS)
    q = q.reshape(B, S, G_HEADS, G_KEY_DIM) * G_KEY_DIM ** -0.5
    k = k.reshape(B, S, G_HEADS, G_KEY_DIM)
    v = v.reshape(B, S, G_HEADS, G_VAL_DIM)
    log_f = (jax.nn.log_sigmoid((lr_f @ w_gate_f + b_gate_f).astype(jnp.float32)) / G_GATE_TAU
             ).reshape(B, S, G_HEADS, G_KEY_DIM)
    log_b = (jax.nn.log_sigmoid((lr_b @ w_gate_b + b_gate_b).astype(jnp.float32)) / G_GATE_TAU
             ).reshape(B, S, G_HEADS, G_KEY_DIM)
    o_f = gla_direction(q, k, v, log_f, True)
    flip = lambda t: jnp.flip(t, axis=1)
    o_b = flip(gla_direction(flip(q), flip(k), flip(v), flip(log_b), False))
    of = o_f + o_b
    of = of * lax.rsqrt(jnp.mean(jnp.square(of), axis=-1, keepdims=True) + RMS_EPS)
    o = (of.reshape(B, S, G_VAL_WIDTH) * head_norm.astype(jnp.float32)).astype(x.dtype)
    return (o * jax.nn.silu(gate)) @ w_out


def setup_inputs(seed: int = 0) -> dict:
    key = jax.random.key(seed)
    ks = jax.random.split(key, 20)
    n = lambda k, shape, scale: jax.random.normal(k, shape, jnp.float32) * scale
    D = D_MODEL
    return {
        "x_prompt": n(ks[0], (BATCH, SEQ, D), 1.0),
        "x_sample": n(ks[1], (DEC_BATCH, DEC_SEQ, D), 1.0),
        "l0_w_in": n(ks[2], (D, sum(A_SPLITS)), D ** -0.5),
        "l0_sink": n(ks[3], (A_HEADS,), 0.5),
        "l0_w_out": n(ks[4], (A_WIDTH, D), A_WIDTH ** -0.5 * DN_BETA),
        "l0_ln_g": 1.0 + n(ks[5], (D,), 0.02),
        "l0_ln_b": n(ks[6], (D,), 0.02),
        "l1_w_in": n(ks[7], (D, sum(G_SPLITS)), D ** -0.5),
        "l1_w_gate_f": n(ks[8], (G_GATE_RANK, G_KEY_WIDTH), G_GATE_RANK ** -0.5),
        "l1_b_gate_f": n(ks[9], (G_KEY_WIDTH,), 0.1),
        "l1_w_gate_b": n(ks[10], (G_GATE_RANK, G_KEY_WIDTH), G_GATE_RANK ** -0.5),
        "l1_b_gate_b": n(ks[11], (G_KEY_WIDTH,), 0.1),
        "l1_head_norm": 1.0 + n(ks[12], (G_VAL_WIDTH,), 0.02),
        "l1_w_out": n(ks[13], (G_VAL_WIDTH, D), G_VAL_WIDTH ** -0.5 * DN_BETA),
        "l1_ln_g": 1.0 + n(ks[14], (D,), 0.02),
        "l1_ln_b": n(ks[15], (D,), 0.02),
    }


def reference(x_prompt, x_sample, l0_w_in, l0_sink, l0_w_out, l0_ln_g, l0_ln_b,
              l1_w_in, l1_w_gate_f, l1_b_gate_f, l1_w_gate_b, l1_b_gate_b, l1_head_norm,
              l1_w_out, l1_ln_g, l1_ln_b):
    mixers = (attn_mixer, gla_mixer)
    layer_params = (
        ((l0_w_in, l0_sink, l0_w_out), l0_ln_g, l0_ln_b),
        ((l1_w_in, l1_w_gate_f, l1_b_gate_f, l1_w_gate_b, l1_b_gate_b, l1_head_norm, l1_w_out),
         l1_ln_g, l1_ln_b),
    )

    def trunk(x):
        for i in range(DEPTH):
            p, g, b = layer_params[i]
            x = layer_norm(DN_ALPHA * x + mixers[i % len(mixers)](x, *p), g, b)
        return x

    y_prompt = trunk(x_prompt)
    y_sample = trunk(x_sample)
    return (y_prompt, y_sample)
```

```python
import functools

import jax
import jax.numpy as jnp
import numpy as np
from jax import lax
from jax.experimental import pallas as pl
from jax.experimental.pallas import tpu as pltpu

F32 = jnp.float32
BF16 = jnp.bfloat16

D_MODEL = 1024
DEPTH = 2
LN_EPS = 1e-5
RMS_EPS = 1e-6
DN_ALPHA = (2 * DEPTH) ** 0.25
NEG = -1e30

A_HEADS = 16
A_KV_HEADS = 4
A_HEAD_DIM = 64
A_REP = A_HEADS // A_KV_HEADS
A_KV_WIDTH = A_KV_HEADS * A_HEAD_DIM
A_BLOCK = 128
A_PAIRS = A_HEADS // 2

G_HEADS = 4
G_KEY_DIM = 128
G_VAL_DIM = 256
G_KEY_WIDTH = G_HEADS * G_KEY_DIM
G_VAL_WIDTH = G_HEADS * G_VAL_DIM
G_RANK = 16
G_TAU = 16.0
G_CHUNK = 128
G_MID = G_CHUNK // 2

PROJ_ROWS = 512
EPI_ROWS = 256
MIB = 1024 * 1024


def _layer_norm(y, g, b):
    mu = jnp.mean(y, axis=-1, keepdims=True)
    yc = y - mu
    var = jnp.mean(yc * yc, axis=-1, keepdims=True)
    return yc * lax.rsqrt(var + LN_EPS) * g + b


def _silu(g):
    return g * (1.0 / (1.0 + jnp.exp(-g)))


def _dot(a, b):
    return jnp.dot(a, b, preferred_element_type=F32)


def _dot_nt(a, b):
    return lax.dot_general(a, b, (((1,), (1,)), ((), ())), preferred_element_type=F32)


def _dot_tn(a, b):
    return lax.dot_general(a, b, (((0,), (0,)), ((), ())), preferred_element_type=F32)


def _l0_proj_kernel(x_ref, w_ref, q_ref, kv_ref, g_ref):
    xb = x_ref[...].astype(BF16)
    half = D_MODEL // 2
    for c in range(2):
        acc = _dot(xb, w_ref[:, c * half:(c + 1) * half])
        q_ref[:, c * half:(c + 1) * half] = (acc * (A_HEAD_DIM ** -0.5)).astype(BF16)
    kv_ref[...] = _dot(xb, w_ref[:, D_MODEL:D_MODEL + 2 * A_KV_WIDTH]).astype(BF16)
    g0 = D_MODEL + 2 * A_KV_WIDTH
    for c in range(2):
        acc = _dot(xb, w_ref[:, g0 + c * half:g0 + (c + 1) * half])
        g_ref[:, c * half:(c + 1) * half] = acc.astype(BF16)


def _l0_proj(x2d, w_bf16):
    m = x2d.shape[0]
    n = w_bf16.shape[1]
    return pl.pallas_call(
        _l0_proj_kernel,
        grid=(m // PROJ_ROWS,),
        in_specs=[pl.BlockSpec((PROJ_ROWS, D_MODEL), lambda i: (i, 0)),
                  pl.BlockSpec((D_MODEL, n), lambda i: (0, 0))],
        out_specs=[pl.BlockSpec((PROJ_ROWS, D_MODEL), lambda i: (i, 0)),
                   pl.BlockSpec((PROJ_ROWS, 2 * A_KV_WIDTH), lambda i: (i, 0)),
                   pl.BlockSpec((PROJ_ROWS, D_MODEL), lambda i: (i, 0))],
        out_shape=[jax.ShapeDtypeStruct((m, D_MODEL), BF16),
                   jax.ShapeDtypeStruct((m, 2 * A_KV_WIDTH), BF16),
                   jax.ShapeDtypeStruct((m, D_MODEL), BF16)],
        compiler_params=pltpu.CompilerParams(dimension_semantics=("arbitrary",),
                                             vmem_limit_bytes=40 * MIB),
        name="l0_proj",
    )(x2d, w_bf16)


def _l0_attn_kernel(sink_ref, q_ref, kvp_ref, kvc_ref, kvn_ref, g_ref, x_ref, bias_ref, edge_ref,
                    wo_ref, lng_ref, lnb_ref, o_ref):
    kwin = jnp.concatenate([kvp_ref[:, :A_KV_WIDTH], kvc_ref[:, :A_KV_WIDTH], kvn_ref[:, :A_KV_WIDTH]], axis=0)
    vwin = jnp.concatenate([kvp_ref[:, A_KV_WIDTH:], kvc_ref[:, A_KV_WIDTH:], kvn_ref[:, A_KV_WIDTH:]], axis=0)
    edge = edge_ref[0]
    zeros = jnp.zeros((3 * A_BLOCK, A_HEAD_DIM), BF16)
    lane = lax.broadcasted_iota(jnp.int32, (A_BLOCK, 2 * A_HEAD_DIM), 1)
    first_half = lane < A_HEAD_DIM
    outs = []
    for g in range(A_KV_HEADS):
        kg = kwin[:, g * A_HEAD_DIM:(g + 1) * A_HEAD_DIM]
        vg = vwin[:, g * A_HEAD_DIM:(g + 1) * A_HEAD_DIM]
        k_lo = jnp.concatenate([kg, zeros], axis=1)
        k_hi = jnp.concatenate([zeros, kg], axis=1)
        v_lo = jnp.concatenate([vg, zeros], axis=1)
        v_hi = jnp.concatenate([zeros, vg], axis=1)
        for pp in range(A_REP // 2):
            pair = g * (A_REP // 2) + pp
            qp = q_ref[:, pair * 128:(pair + 1) * 128]
            acc = None
            rls = []
            for hh, (kk, vv) in enumerate(((k_lo, v_lo), (k_hi, v_hi))):
                h = 2 * pair + hh
                s = _dot_nt(qp, kk) + bias_ref[h] + edge
                sink = sink_ref[h]
                m = jnp.maximum(jnp.max(s, axis=-1, keepdims=True), sink)
                p = jnp.exp(s - m)
                l = jnp.sum(p, axis=-1, keepdims=True) + jnp.exp(sink - m)
                rls.append(1.0 / l)
                pv = _dot(p.astype(BF16), vv)
                acc = pv if acc is None else acc + pv
            outs.append(acc * jnp.where(first_half, rls[0], rls[1]))
    o = jnp.concatenate(outs, axis=1)
    a = (o * _silu(g_ref[...].astype(F32))).astype(BF16)
    y = DN_ALPHA * x_ref[...] + _dot(a, wo_ref[...])
    o_ref[...] = _layer_norm(y, lng_ref[...], lnb_ref[...])


def _l0_attn(q, kv, gate, x2d, sink, bias, edge, wo_bf16, ln_g, ln_b, n_seq, n_blk):
    m = x2d.shape[0]
    row = lambda b, n: (b * n_blk + n, 0)
    prev = lambda b, n: (b * n_blk + jnp.maximum(n - 1, 0), 0)
    nxt = lambda b, n: (b * n_blk + jnp.minimum(n + 1, n_blk - 1), 0)
    const2 = lambda b, n: (0, 0)
    return pl.pallas_call(
        _l0_attn_kernel,
        grid=(n_seq, n_blk),
        in_specs=[pl.BlockSpec(memory_space=pltpu.SMEM),
                  pl.BlockSpec((A_BLOCK, D_MODEL), row),
                  pl.BlockSpec((A_BLOCK, 2 * A_KV_WIDTH), prev),
                  pl.BlockSpec((A_BLOCK, 2 * A_KV_WIDTH), row),
                  pl.BlockSpec((A_BLOCK, 2 * A_KV_WIDTH), nxt),
                  pl.BlockSpec((A_BLOCK, D_MODEL), row),
                  pl.BlockSpec((A_BLOCK, D_MODEL), row),
                  pl.BlockSpec((A_HEADS, A_BLOCK, 3 * A_BLOCK), lambda b, n: (0, 0, 0)),
                  pl.BlockSpec((1, 1, 3 * A_BLOCK), lambda b, n: (n, 0, 0)),
                  pl.BlockSpec((D_MODEL, D_MODEL), const2),
                  pl.BlockSpec((1, D_MODEL), const2),
                  pl.BlockSpec((1, D_MODEL), const2)],
        out_specs=pl.BlockSpec((A_BLOCK, D_MODEL), row),
        out_shape=jax.ShapeDtypeStruct((m, D_MODEL), F32),
        compiler_params=pltpu.CompilerParams(dimension_semantics=("arbitrary", "arbitrary"),
                                             vmem_limit_bytes=40 * MIB),
        name="l0_attn",
    )(sink, q, kv, kv, kv, gate, x2d, bias, edge, wo_bf16, ln_g, ln_b)


def _attn_tables(n_blk):
    qi = np.arange(A_BLOCK)[:, None] + A_BLOCK
    kj = np.arange(3 * A_BLOCK)[None, :]
    dist = np.abs(qi - kj).astype(np.float32)
    in_win = dist <= A_BLOCK
    slopes = (2.0 ** (-8.0 * np.arange(1, A_HEADS + 1) / A_HEADS)).astype(np.float32)
    bias = np.where(in_win[None], -slopes[:, None, None] * dist[None], np.float32(NEG)).astype(np.float32)
    edge = np.zeros((n_blk, 1, 3 * A_BLOCK), np.float32)
    edge[0, 0, :A_BLOCK] = NEG
    edge[n_blk - 1, 0, 2 * A_BLOCK:] = NEG
    return jnp.asarray(bias), jnp.asarray(edge)


def _log_sigmoid(z):
    return jnp.minimum(z, 0.0) - jnp.log1p(jnp.exp(-jnp.abs(z)))


def _split_cumsum(tri_bf16, la):
    hi = la.astype(BF16)
    lo = (la - hi.astype(F32)).astype(BF16)
    return _dot(tri_bf16, hi) + _dot(tri_bf16, lo)


def _l1_proj_kernel(x_ref, w_ref, wlr_ref, wg_ref, bg_ref, tril_ref, triu_ref,
                    p_ref, v_ref, g_ref, ed_ref):
    xb = x_ref[...].astype(BF16)
    kw = G_KEY_WIDTH
    q = _dot(xb, w_ref[:, 0:kw]) * (G_KEY_DIM ** -0.5)
    k = _dot(xb, w_ref[:, kw:2 * kw])
    for c in range(2):
        c0 = 2 * kw + c * kw
        v_ref[:, c * kw:(c + 1) * kw] = _dot(xb, w_ref[:, c0:c0 + kw]).astype(BF16)
    for c in range(2):
        c0 = 2 * kw + G_VAL_WIDTH + c * kw
        g_ref[:, c * kw:(c + 1) * kw] = _dot(xb, w_ref[:, c0:c0 + kw]).astype(BF16)
    lr = _dot(xb, wlr_ref[...]).astype(BF16)
    z = _dot(lr, wg_ref[...]) + bg_ref[...]
    la = _log_sigmoid(z) * (1.0 / G_TAU)
    for c in range(PROJ_ROWS // G_CHUNK):
        r0 = c * G_CHUNK
        qc = q[r0:r0 + G_CHUNK]
        kc = k[r0:r0 + G_CHUNK]
        b = _split_cumsum(tril_ref[...], la[r0:r0 + G_CHUNK, 0:kw])
        b_mid = b[G_MID - 1:G_MID]
        b_end = b[G_CHUNK - 1:G_CHUNK]
        p_ref[r0:r0 + G_CHUNK, 0:kw] = (qc * jnp.exp(b - b_mid)).astype(BF16)
        p_ref[r0:r0 + G_CHUNK, kw:2 * kw] = (kc * jnp.exp(b_mid - b)).astype(BF16)
        p_ref[r0:r0 + G_CHUNK, 2 * kw:3 * kw] = (kc * jnp.exp(b_end - b)).astype(BF16)
        ed_ref[c, 0:1, :] = jnp.exp(b_mid)
        ed_ref[c, 1:2, :] = jnp.exp(b_end)
        s = _split_cumsum(triu_ref[...], la[r0:r0 + G_CHUNK, kw:2 * kw])
        s_mid = s[G_MID:G_MID + 1]
        s_end = s[0:1]
        p_ref[r0:r0 + G_CHUNK, 3 * kw:4 * kw] = (qc * jnp.exp(s - s_mid)).astype(BF16)
        p_ref[r0:r0 + G_CHUNK, 4 * kw:5 * kw] = (kc * jnp.exp(s_mid - s)).astype(BF16)
        p_ref[r0:r0 + G_CHUNK, 5 * kw:6 * kw] = (kc * jnp.exp(s_end - s)).astype(BF16)
        ed_ref[c, 2:3, :] = jnp.exp(s_mid)
        ed_ref[c, 3:4, :] = jnp.exp(s_end)


def _l1_proj(x2d, w_bf16, wlr, wg, bg, tril, triu):
    m = x2d.shape[0]
    n_main = w_bf16.shape[1]
    cpt = PROJ_ROWS // G_CHUNK
    const2 = lambda i: (0, 0)
    return pl.pallas_call(
        _l1_proj_kernel,
        grid=(m // PROJ_ROWS,),
        in_specs=[pl.BlockSpec((PROJ_ROWS, D_MODEL), lambda i: (i, 0)),
                  pl.BlockSpec((D_MODEL, n_main), const2),
                  pl.BlockSpec((D_MODEL, 128), const2),
                  pl.BlockSpec((128, 2 * G_KEY_WIDTH), const2),
                  pl.BlockSpec((1, 2 * G_KEY_WIDTH), const2),
                  pl.BlockSpec((G_CHUNK, G_CHUNK), const2),
                  pl.BlockSpec((G_CHUNK, G_CHUNK), const2)],
        out_specs=[pl.BlockSpec((PROJ_ROWS, 6 * G_KEY_WIDTH), lambda i: (i, 0)),
                   pl.BlockSpec((PROJ_ROWS, G_VAL_WIDTH), lambda i: (i, 0)),
                   pl.BlockSpec((PROJ_ROWS, G_VAL_WIDTH), lambda i: (i, 0)),
                   pl.BlockSpec((cpt, 4, G_KEY_WIDTH), lambda i: (i, 0, 0))],
        out_shape=[jax.ShapeDtypeStruct((m, 6 * G_KEY_WIDTH), BF16),
                   jax.ShapeDtypeStruct((m, G_VAL_WIDTH), BF16),
                   jax.ShapeDtypeStruct((m, G_VAL_WIDTH), BF16),
                   jax.ShapeDtypeStruct((m // G_CHUNK, 4, G_KEY_WIDTH), F32)],
        compiler_params=pltpu.CompilerParams(dimension_semantics=("arbitrary",),
                                             vmem_limit_bytes=56 * MIB),
        name="l1_proj",
    )(x2d, w_bf16, wlr, wg, bg, tril, triu)


def _gla_scan_kernel(pf_ref, pb_ref, vf_ref, vb_ref, edf_ref, edb_ref, of_ref, ob_ref, state_ref):
    @pl.when(pl.program_id(1) == 0)
    def _():
        state_ref[...] = jnp.zeros_like(state_ref)

    rows = lax.broadcasted_iota(jnp.int32, (G_CHUNK, G_CHUNK), 0)
    cols = lax.broadcasted_iota(jnp.int32, (G_CHUNK, G_CHUNK), 1)
    eye = (rows == cols).astype(F32)
    dirs = ((pf_ref, vf_ref, edf_ref, of_ref, rows >= cols, 0),
            (pb_ref, vb_ref, edb_ref, ob_ref, rows < cols, 1))
    kw = G_KEY_WIDTH
    for p_ref, v_ref, ed_ref, o_ref, mask, d in dirs:
        for h in range(G_HEADS):
            c0 = h * G_KEY_DIM
            qd = p_ref[:, c0:c0 + G_KEY_DIM]
            ki = p_ref[:, kw + c0:kw + c0 + G_KEY_DIM]
            kt = p_ref[:, 2 * kw + c0:2 * kw + c0 + G_KEY_DIM]
            vh = v_ref[:, h * G_VAL_DIM:(h + 1) * G_VAL_DIM]
            e_row = ed_ref[0, 2 * d:2 * d + 1, c0:c0 + G_KEY_DIM]
            d_row = ed_ref[0, 2 * d + 1:2 * d + 2, c0:c0 + G_KEY_DIM]
            e_col = jnp.sum(eye * e_row, axis=1, keepdims=True)
            d_col = jnp.sum(eye * d_row, axis=1, keepdims=True)
            st = state_ref[d * G_HEADS + h]
            att = jnp.where(mask, _dot_nt(qd, ki), 0.0).astype(BF16)
            o = _dot(att, vh) + _dot(qd, (st * e_col).astype(BF16))
            o_ref[:, h * G_VAL_DIM:(h + 1) * G_VAL_DIM] = o.astype(BF16)
            state_ref[d * G_HEADS + h] = st * d_col + _dot_tn(kt, vh)


def _gla_scan(p, v, ed, n_seq, n_chunk):
    m = v.shape[0]
    pw = 3 * G_KEY_WIDTH
    fwd = lambda b, j: (b * n_chunk + j, 0)
    bwd = lambda b, j: (b * n_chunk + n_chunk - 1 - j, 0)
    bwd_p = lambda b, j: (b * n_chunk + n_chunk - 1 - j, 1)
    fwd3 = lambda b, j: (b * n_chunk + j, 0, 0)
    bwd3 = lambda b, j: (b * n_chunk + n_chunk - 1 - j, 0, 0)
    return pl.pallas_call(
        _gla_scan_kernel,
        grid=(n_seq, n_chunk),
        in_specs=[pl.BlockSpec((G_CHUNK, pw), fwd),
                  pl.BlockSpec((G_CHUNK, pw), bwd_p),
                  pl.BlockSpec((G_CHUNK, G_VAL_WIDTH), fwd),
                  pl.BlockSpec((G_CHUNK, G_VAL_WIDTH), bwd),
                  pl.BlockSpec((1, 4, G_KEY_WIDTH), fwd3),
                  pl.BlockSpec((1, 4, G_KEY_WIDTH), bwd3)],
        out_specs=[pl.BlockSpec((G_CHUNK, G_VAL_WIDTH), fwd),
                   pl.BlockSpec((G_CHUNK, G_VAL_WIDTH), bwd)],
        out_shape=[jax.ShapeDtypeStruct((m, G_VAL_WIDTH), BF16),
                   jax.ShapeDtypeStruct((m, G_VAL_WIDTH), BF16)],
        scratch_shapes=[pltpu.VMEM((2 * G_HEADS, G_KEY_DIM, G_VAL_DIM), F32)],
        compiler_params=pltpu.CompilerParams(dimension_semantics=("arbitrary", "arbitrary"),
                                             vmem_limit_bytes=32 * MIB),
        name="gla_scan",
    )(p, p, v, v, ed, ed)


def _l1_epilogue_kernel(of_ref, ob_ref, g_ref, x_ref, hn_ref, wo_ref, lng_ref, lnb_ref, o_ref):
    o = of_ref[...].astype(F32) + ob_ref[...].astype(F32)
    segs = []
    for h in range(G_HEADS):
        seg = o[:, h * G_VAL_DIM:(h + 1) * G_VAL_DIM]
        ms = jnp.mean(seg * seg, axis=-1, keepdims=True)
        segs.append(seg * lax.rsqrt(ms + RMS_EPS))
    on = jnp.concatenate(segs, axis=1) * hn_ref[...]
    a = (on * _silu(g_ref[...].astype(F32))).astype(BF16)
    y = DN_ALPHA * x_ref[...] + _dot(a, wo_ref[...])
    o_ref[...] = _layer_norm(y, lng_ref[...], lnb_ref[...])


def _l1_epilogue(o_f, o_b, gate, x2d, head_norm, wo_bf16, ln_g, ln_b):
    m = x2d.shape[0]
    row = lambda i: (i, 0)
    const2 = lambda i: (0, 0)
    return pl.pallas_call(
        _l1_epilogue_kernel,
        grid=(m // EPI_ROWS,),
        in_specs=[pl.BlockSpec((EPI_ROWS, G_VAL_WIDTH), row),
                  pl.BlockSpec((EPI_ROWS, G_VAL_WIDTH), row),
                  pl.BlockSpec((EPI_ROWS, G_VAL_WIDTH), row),
                  pl.BlockSpec((EPI_ROWS, D_MODEL), row),
                  pl.BlockSpec((1, G_VAL_WIDTH), const2),
                  pl.BlockSpec((G_VAL_WIDTH, D_MODEL), const2),
                  pl.BlockSpec((1, D_MODEL), const2),
                  pl.BlockSpec((1, D_MODEL), const2)],
        out_specs=pl.BlockSpec((EPI_ROWS, D_MODEL), row),
        out_shape=jax.ShapeDtypeStruct((m, D_MODEL), F32),
        compiler_params=pltpu.CompilerParams(dimension_semantics=("arbitrary",),
                                             vmem_limit_bytes=32 * MIB),
        name="l1_epilogue",
    )(o_f, o_b, gate, x2d, head_norm, wo_bf16, ln_g, ln_b)


def _trunk(x, l0, l1):
    n_seq, seq, _ = x.shape
    assert seq % PROJ_ROWS == 0 and seq % A_BLOCK == 0 and seq % G_CHUNK == 0
    x2d = x.reshape(n_seq * seq, D_MODEL)
    q, kv, gate = _l0_proj(x2d, l0["w_in"])
    x1 = _l0_attn(q, kv, gate, x2d, l0["sink"], l0["bias"], l0["edge"], l0["w_out"],
                  l0["ln_g"], l0["ln_b"], n_seq, seq // A_BLOCK)
    p, v, gate1, ed = _l1_proj(x1, l1["w_in"], l1["w_lr"], l1["w_gate"], l1["b_gate"], l1["tril"], l1["triu"])
    o_f, o_b = _gla_scan(p, v, ed, n_seq, seq // G_CHUNK)
    y = _l1_epilogue(o_f, o_b, gate1, x1, l1["head_norm"], l1["w_out"], l1["ln_g"], l1["ln_b"])
    return y.reshape(n_seq, seq, D_MODEL)


def kernel(x_prompt, x_sample, l0_w_in, l0_sink, l0_w_out, l0_ln_g, l0_ln_b,
           l1_w_in, l1_w_gate_f, l1_b_gate_f, l1_w_gate_b, l1_b_gate_b, l1_head_norm,
           l1_w_out, l1_ln_g, l1_ln_b):
    assert x_prompt.shape[1] == x_sample.shape[1]
    bias, edge = _attn_tables(x_prompt.shape[1] // A_BLOCK)
    row = lambda t: t.reshape(1, -1).astype(F32)
    l0 = dict(w_in=l0_w_in.astype(BF16), sink=l0_sink.astype(F32), bias=bias, edge=edge,
              w_out=l0_w_out.astype(BF16), ln_g=row(l0_ln_g), ln_b=row(l0_ln_b))
    n_main = 2 * G_KEY_WIDTH + 2 * G_VAL_WIDTH
    w_lr = jnp.zeros((D_MODEL, 128), BF16).at[:, :2 * G_RANK].set(l1_w_in[:, n_main:].astype(BF16))
    w_gate = jnp.zeros((128, 2 * G_KEY_WIDTH), BF16)
    w_gate = w_gate.at[:G_RANK, :G_KEY_WIDTH].set(l1_w_gate_f.astype(BF16))
    w_gate = w_gate.at[G_RANK:2 * G_RANK, G_KEY_WIDTH:].set(l1_w_gate_b.astype(BF16))
    b_gate = jnp.concatenate([l1_b_gate_f, l1_b_gate_b]).reshape(1, -1).astype(F32)
    tri = np.tril(np.ones((G_CHUNK, G_CHUNK), np.float32))
    l1 = dict(w_in=l1_w_in[:, :n_main].astype(BF16), w_lr=w_lr, w_gate=w_gate, b_gate=b_gate,
              tril=jnp.asarray(tri, BF16), triu=jnp.asarray(tri.T, BF16),
              head_norm=row(l1_head_norm), w_out=l1_w_out.astype(BF16),
              ln_g=row(l1_ln_g), ln_b=row(l1_ln_b))
    return (_trunk(x_prompt, l0, l1), _trunk(x_sample, l0, l1))
```

```python
import jax
import jax.numpy as jnp
import numpy as np
from jax import lax
from jax.experimental import pallas as pl
from jax.experimental.pallas import tpu as pltpu

F32 = jnp.float32
BF16 = jnp.bfloat16

D_MODEL = 1024
DEPTH = 2
LN_EPS = 1e-5
RMS_EPS = 1e-6
DN_ALPHA = (2 * DEPTH) ** 0.25
NEG = -1e30
LOG2E = 1.4426950408889634

A_HEADS = 16
A_KV_HEADS = 4
A_HEAD_DIM = 64
A_REP = A_HEADS // A_KV_HEADS
A_KV_WIDTH = A_KV_HEADS * A_HEAD_DIM
A_BLOCK = 128
A_PAIRS = A_HEADS // 2
A_WIN = 3 * A_BLOCK
A_QK_AHEAD = 3

G_HEADS = 4
G_KEY_DIM = 128
G_VAL_DIM = 256
G_KEY_WIDTH = G_HEADS * G_KEY_DIM
G_VAL_WIDTH = G_HEADS * G_VAL_DIM
G_RANK = 16
G_TAU = 16.0
G_CHUNK = 128
G_MID = G_CHUNK // 2

PROJ_ROWS = 512
EPI_ROWS = 256
MIB = 1024 * 1024


def _layer_norm(y, g, b):
    mu = jnp.mean(y, axis=-1, keepdims=True)
    yc = y - mu
    var = jnp.mean(yc * yc, axis=-1, keepdims=True)
    return yc * lax.rsqrt(var + LN_EPS) * g + b


def _silu(g):
    return g * (1.0 / (1.0 + jnp.exp(-g)))


def _dot(a, b):
    return jnp.dot(a, b, preferred_element_type=F32)


def _dot_nt(a, b):
    return lax.dot_general(a, b, (((1,), (1,)), ((), ())), preferred_element_type=F32)


def _dot_tn(a, b):
    return lax.dot_general(a, b, (((0,), (0,)), ((), ())), preferred_element_type=F32)


def _l0_proj_kernel(x_ref, w_ref, q_ref, kv_ref, g_ref):
    xb = x_ref[...].astype(BF16)
    half = D_MODEL // 2
    for c in range(2):
        acc = _dot(xb, w_ref[:, c * half:(c + 1) * half])
        q_ref[:, c * half:(c + 1) * half] = (acc * (A_HEAD_DIM ** -0.5 * LOG2E)).astype(BF16)
    kv_ref[...] = _dot(xb, w_ref[:, D_MODEL:D_MODEL + 2 * A_KV_WIDTH]).astype(BF16)
    g0 = D_MODEL + 2 * A_KV_WIDTH
    for c in range(2):
        acc = _dot(xb, w_ref[:, g0 + c * half:g0 + (c + 1) * half])
        g_ref[:, c * half:(c + 1) * half] = acc.astype(BF16)


def _l0_proj(x2d, w_bf16):
    m = x2d.shape[0]
    n = w_bf16.shape[1]
    return pl.pallas_call(
        _l0_proj_kernel,
        grid=(m // PROJ_ROWS,),
        in_specs=[pl.BlockSpec((PROJ_ROWS, D_MODEL), lambda i: (i, 0)),
                  pl.BlockSpec((D_MODEL, n), lambda i: (0, 0))],
        out_specs=[pl.BlockSpec((PROJ_ROWS, D_MODEL), lambda i: (i, 0)),
                   pl.BlockSpec((PROJ_ROWS, 2 * A_KV_WIDTH), lambda i: (i, 0)),
                   pl.BlockSpec((PROJ_ROWS, D_MODEL), lambda i: (i, 0))],
        out_shape=[jax.ShapeDtypeStruct((m, D_MODEL), BF16),
                   jax.ShapeDtypeStruct((m, 2 * A_KV_WIDTH), BF16),
                   jax.ShapeDtypeStruct((m, D_MODEL), BF16)],
        compiler_params=pltpu.CompilerParams(dimension_semantics=("arbitrary",),
                                             vmem_limit_bytes=40 * MIB),
        name="l0_proj",
    )(x2d, w_bf16)


def _l0_attn_kernel(sink_ref, q_ref, kvp_ref, kvc_ref, kvn_ref, g_ref, x_ref, bias_ref,
                    wo_ref, lng_ref, lnb_ref, o_ref, a_ref):
    @pl.when((pl.program_id(0) == 0) & (pl.program_id(1) == 0))
    def _():
        a_ref[...] = jnp.zeros_like(a_ref)

    y = DN_ALPHA * x_ref[...] + _dot(a_ref[...], wo_ref[...])
    o_ref[...] = _layer_norm(y, lng_ref[...], lnb_ref[...])

    kwin = jnp.concatenate([kvp_ref[:, :A_KV_WIDTH], kvc_ref[:, :A_KV_WIDTH], kvn_ref[:, :A_KV_WIDTH]], axis=0)
    vwin = jnp.concatenate([kvp_ref[:, A_KV_WIDTH:], kvc_ref[:, A_KV_WIDTH:], kvn_ref[:, A_KV_WIDTH:]], axis=0)
    zeros = jnp.zeros((A_WIN, A_HEAD_DIM), BF16)
    lane = lax.broadcasted_iota(jnp.int32, (A_BLOCK, 2 * A_HEAD_DIM), 1)
    first_half = lane < A_HEAD_DIM
    kk, vv = [], []
    for g in range(A_KV_HEADS):
        kg = kwin[:, g * A_HEAD_DIM:(g + 1) * A_HEAD_DIM]
        vg = vwin[:, g * A_HEAD_DIM:(g + 1) * A_HEAD_DIM]
        kk.append(jnp.concatenate([jnp.concatenate([kg, zeros], axis=1),
                                   jnp.concatenate([zeros, kg], axis=1)], axis=0))
        vv.append(jnp.concatenate([jnp.concatenate([vg, zeros], axis=1),
                                   jnp.concatenate([zeros, vg], axis=1)], axis=0))

    def scores(i):
        return _dot_nt(q_ref[:, i * 128:(i + 1) * 128], kk[i // (A_REP // 2)])

    def softmax(i, s):
        t = s + bias_ref[0, i]
        ps, rls = [], []
        for hh in range(2):
            th = t[:, hh * A_WIN:(hh + 1) * A_WIN]
            m = jnp.max(th, axis=-1, keepdims=True)
            p = jnp.exp2(th - m)
            l = jnp.sum(p, axis=-1, keepdims=True) + jnp.exp2(sink_ref[2 * i + hh] * LOG2E - m)
            rls.append(1.0 / l)
            ps.append(p.astype(BF16))
        return jnp.concatenate(ps, axis=1), jnp.where(first_half, rls[0], rls[1])

    def weighted_values(i, p, r):
        return _dot(p, vv[i // (A_REP // 2)]) * r

    s_ready = {i: scores(i) for i in range(min(A_QK_AHEAD, A_PAIRS))}
    outs, pending = [], None
    for i in range(A_PAIRS):
        if i + A_QK_AHEAD < A_PAIRS:
            s_ready[i + A_QK_AHEAD] = scores(i + A_QK_AHEAD)
        p, r = softmax(i, s_ready.pop(i))
        if pending is not None:
            outs.append(weighted_values(*pending))
        pending = (i, p, r)
    outs.append(weighted_values(*pending))
    o = jnp.concatenate(outs, axis=1)
    a_ref[...] = (o * _silu(g_ref[...].astype(F32))).astype(BF16)


def _l0_attn(q, kv, gate, x2d, sink, bias, wo_bf16, ln_g, ln_b, n_seq, n_blk):
    m = x2d.shape[0]
    blk = lambda n: jnp.minimum(n, n_blk - 1)
    row = lambda b, n: (b * n_blk + blk(n), 0)
    prev = lambda b, n: (b * n_blk + jnp.maximum(blk(n) - 1, 0), 0)
    nxt = lambda b, n: (b * n_blk + jnp.minimum(blk(n) + 1, n_blk - 1), 0)
    done = lambda b, n: (b * n_blk + jnp.maximum(n - 1, 0), 0)
    const2 = lambda b, n: (0, 0)
    variant = lambda b, n: (jnp.where(n == 0, 0, jnp.where(n >= n_blk - 1, 2, 1)), 0, 0, 0)
    return pl.pallas_call(
        _l0_attn_kernel,
        grid=(n_seq, n_blk + 1),
        in_specs=[pl.BlockSpec(memory_space=pltpu.SMEM),
                  pl.BlockSpec((A_BLOCK, D_MODEL), row),
                  pl.BlockSpec((A_BLOCK, 2 * A_KV_WIDTH), prev),
                  pl.BlockSpec((A_BLOCK, 2 * A_KV_WIDTH), row),
                  pl.BlockSpec((A_BLOCK, 2 * A_KV_WIDTH), nxt),
                  pl.BlockSpec((A_BLOCK, D_MODEL), row),
                  pl.BlockSpec((A_BLOCK, D_MODEL), done),
                  pl.BlockSpec((1, A_PAIRS, A_BLOCK, 2 * A_WIN), variant),
                  pl.BlockSpec((D_MODEL, D_MODEL), const2),
                  pl.BlockSpec((1, D_MODEL), const2),
                  pl.BlockSpec((1, D_MODEL), const2)],
        out_specs=pl.BlockSpec((A_BLOCK, D_MODEL), done),
        out_shape=jax.ShapeDtypeStruct((m, D_MODEL), F32),
        scratch_shapes=[pltpu.VMEM((A_BLOCK, D_MODEL), BF16)],
        compiler_params=pltpu.CompilerParams(dimension_semantics=("arbitrary", "arbitrary"),
                                             vmem_limit_bytes=40 * MIB),
        name="l0_attn",
    )(sink, q, kv, kv, kv, gate, x2d, bias, wo_bf16, ln_g, ln_b)


def _attn_bias_table():
    qi = np.arange(A_BLOCK)[:, None] + A_BLOCK
    kj = np.arange(A_WIN)[None, :]
    dist = np.abs(qi - kj).astype(np.float32)
    in_win = dist <= A_BLOCK
    slopes = (2.0 ** (-8.0 * np.arange(1, A_HEADS + 1) / A_HEADS)).astype(np.float32)
    per_head = (-slopes[:, None, None] * dist[None]).astype(np.float32) * np.float32(LOG2E)
    variants = []
    for valid in (kj >= A_BLOCK, kj >= 0, kj < 2 * A_BLOCK):
        b = np.where((in_win & valid)[None], per_head, np.float32(NEG)).astype(np.float32)
        variants.append(b.reshape(A_PAIRS, 2, A_BLOCK, A_WIN).transpose(0, 2, 1, 3).reshape(A_PAIRS, A_BLOCK, 2 * A_WIN))
    return jnp.asarray(np.stack(variants))


def _log2_decay(z):
    return (jnp.minimum(z, 0.0) - jnp.log(1.0 + jnp.exp(-jnp.abs(z)))) * (LOG2E / G_TAU)


def _split_cumsum(tri2_bf16, la):
    hi = la.astype(BF16)
    lo = (la - hi.astype(F32)).astype(BF16)
    return _dot(tri2_bf16, jnp.concatenate([hi, lo], axis=0))


def _l1_proj_kernel(x_ref, w_ref, wlr_ref, wg_ref, bg_ref, tril_ref, triu_ref,
                    p_ref, v_ref, g_ref, ed_ref):
    xb = x_ref[...].astype(BF16)
    kw = G_KEY_WIDTH
    lr = _dot(xb, wlr_ref[...]).astype(BF16)
    z = _dot(lr, wg_ref[...]) + bg_ref[...]
    q = _dot(xb, w_ref[:, 0:kw]) * (G_KEY_DIM ** -0.5)
    k = _dot(xb, w_ref[:, kw:2 * kw])
    la = _log2_decay(z)
    n_chunks = PROJ_ROWS // G_CHUNK
    cums = [(_split_cumsum(tril_ref[...], la[c * G_CHUNK:(c + 1) * G_CHUNK, 0:kw]),
             _split_cumsum(triu_ref[...], la[c * G_CHUNK:(c + 1) * G_CHUNK, kw:2 * kw]))
            for c in range(n_chunks)]
    for c in range(2):
        c0 = 2 * kw + c * kw
        v_ref[:, c * kw:(c + 1) * kw] = _dot(xb, w_ref[:, c0:c0 + kw]).astype(BF16)
    for c in range(n_chunks):
        r0 = c * G_CHUNK
        qc = q[r0:r0 + G_CHUNK]
        kc = k[r0:r0 + G_CHUNK]
        b, s = cums[c]
        b_mid = b[G_MID - 1:G_MID]
        b_end = b[G_CHUNK - 1:G_CHUNK]
        p_ref[r0:r0 + G_CHUNK, 0:kw] = (qc * jnp.exp2(b - b_mid)).astype(BF16)
        p_ref[r0:r0 + G_CHUNK, kw:2 * kw] = (kc * jnp.exp2(b_mid - b)).astype(BF16)
        p_ref[r0:r0 + G_CHUNK, 2 * kw:3 * kw] = (kc * jnp.exp2(b_end - b)).astype(BF16)
        ed_ref[c, 0:1, :] = jnp.exp2(b_mid)
        ed_ref[c, 1:2, :] = jnp.exp2(b_end)
        s_mid = s[G_MID:G_MID + 1]
        s_end = s[0:1]
        p_ref[r0:r0 + G_CHUNK, 3 * kw:4 * kw] = (qc * jnp.exp2(s - s_mid)).astype(BF16)
        p_ref[r0:r0 + G_CHUNK, 4 * kw:5 * kw] = (kc * jnp.exp2(s_mid - s)).astype(BF16)
        p_ref[r0:r0 + G_CHUNK, 5 * kw:6 * kw] = (kc * jnp.exp2(s_end - s)).astype(BF16)
        ed_ref[c, 2:3, :] = jnp.exp2(s_mid)
        ed_ref[c, 3:4, :] = jnp.exp2(s_end)
    for c in range(2):
        c0 = 2 * kw + G_VAL_WIDTH + c * kw
        g_ref[:, c * kw:(c + 1) * kw] = _dot(xb, w_ref[:, c0:c0 + kw]).astype(BF16)


def _l1_proj(x2d, w_bf16, wlr, wg, bg, tril, triu):
    m = x2d.shape[0]
    n_main = w_bf16.shape[1]
    cpt = PROJ_ROWS // G_CHUNK
    const2 = lambda i: (0, 0)
    return pl.pallas_call(
        _l1_proj_kernel,
        grid=(m // PROJ_ROWS,),
        in_specs=[pl.BlockSpec((PROJ_ROWS, D_MODEL), lambda i: (i, 0)),
                  pl.BlockSpec((D_MODEL, n_main), const2),
                  pl.BlockSpec((D_MODEL, 128), const2),
                  pl.BlockSpec((128, 2 * G_KEY_WIDTH), const2),
                  pl.BlockSpec((1, 2 * G_KEY_WIDTH), const2),
                  pl.BlockSpec((G_CHUNK, 2 * G_CHUNK), const2),
                  pl.BlockSpec((G_CHUNK, 2 * G_CHUNK), const2)],
        out_specs=[pl.BlockSpec((PROJ_ROWS, 6 * G_KEY_WIDTH), lambda i: (i, 0)),
                   pl.BlockSpec((PROJ_ROWS, G_VAL_WIDTH), lambda i: (i, 0)),
                   pl.BlockSpec((PROJ_ROWS, G_VAL_WIDTH), lambda i: (i, 0)),
                   pl.BlockSpec((cpt, 4, G_KEY_WIDTH), lambda i: (i, 0, 0))],
        out_shape=[jax.ShapeDtypeStruct((m, 6 * G_KEY_WIDTH), BF16),
                   jax.ShapeDtypeStruct((m, G_VAL_WIDTH), BF16),
                   jax.ShapeDtypeStruct((m, G_VAL_WIDTH), BF16),
                   jax.ShapeDtypeStruct((m // G_CHUNK, 4, G_KEY_WIDTH), F32)],
        compiler_params=pltpu.CompilerParams(dimension_semantics=("arbitrary",),
                                             vmem_limit_bytes=56 * MIB),
        name="l1_proj",
    )(x2d, w_bf16, wlr, wg, bg, tril, triu)


def _gla_scan_kernel(pf_ref, pb_ref, vf_ref, vb_ref, edf_ref, edb_ref, of_ref, ob_ref, state_ref):
    @pl.when(pl.program_id(1) == 0)
    def _():
        state_ref[...] = jnp.zeros_like(state_ref)

    rows = lax.broadcasted_iota(jnp.int32, (G_CHUNK, G_CHUNK), 0)
    cols = lax.broadcasted_iota(jnp.int32, (G_CHUNK, G_CHUNK), 1)
    eye = (rows == cols).astype(F32)
    dirs = ((pf_ref, vf_ref, edf_ref, of_ref, rows >= cols, 0),
            (pb_ref, vb_ref, edb_ref, ob_ref, rows < cols, 1))
    kw = G_KEY_WIDTH
    jobs = [(p_ref, v_ref, ed_ref, o_ref, mask, d, h)
            for p_ref, v_ref, ed_ref, o_ref, mask, d in dirs for h in range(G_HEADS)]
    atts, inters = [], []
    for p_ref, v_ref, ed_ref, o_ref, mask, d, h in jobs:
        c0 = h * G_KEY_DIM
        qd = p_ref[:, c0:c0 + G_KEY_DIM]
        ki = p_ref[:, kw + c0:kw + c0 + G_KEY_DIM]
        atts.append(jnp.where(mask, _dot_nt(qd, ki), 0.0).astype(BF16))
    for p_ref, v_ref, ed_ref, o_ref, mask, d, h in jobs:
        c0 = h * G_KEY_DIM
        qd = p_ref[:, c0:c0 + G_KEY_DIM]
        kt = p_ref[:, 2 * kw + c0:2 * kw + c0 + G_KEY_DIM]
        vh = v_ref[:, h * G_VAL_DIM:(h + 1) * G_VAL_DIM]
        e_row = ed_ref[0, 2 * d:2 * d + 1, c0:c0 + G_KEY_DIM]
        d_row = ed_ref[0, 2 * d + 1:2 * d + 2, c0:c0 + G_KEY_DIM]
        e_col = jnp.sum(eye * e_row, axis=1, keepdims=True)
        d_col = jnp.sum(eye * d_row, axis=1, keepdims=True)
        st = state_ref[d * G_HEADS + h]
        inters.append(_dot(qd, (st * e_col).astype(BF16)))
        state_ref[d * G_HEADS + h] = st * d_col + _dot_tn(kt, vh)
    for (p_ref, v_ref, ed_ref, o_ref, mask, d, h), att, inter in zip(jobs, atts, inters):
        vh = v_ref[:, h * G_VAL_DIM:(h + 1) * G_VAL_DIM]
        o_ref[:, h * G_VAL_DIM:(h + 1) * G_VAL_DIM] = (_dot(att, vh) + inter).astype(BF16)


def _gla_scan(p, v, ed, n_seq, n_chunk):
    m = v.shape[0]
    pw = 3 * G_KEY_WIDTH
    fwd = lambda b, j: (b * n_chunk + j, 0)
    bwd = lambda b, j: (b * n_chunk + n_chunk - 1 - j, 0)
    bwd_p = lambda b, j: (b * n_chunk + n_chunk - 1 - j, 1)
    fwd3 = lambda b, j: (b * n_chunk + j, 0, 0)
    bwd3 = lambda b, j: (b * n_chunk + n_chunk - 1 - j, 0, 0)
    return pl.pallas_call(
        _gla_scan_kernel,
        grid=(n_seq, n_chunk),
        in_specs=[pl.BlockSpec((G_CHUNK, pw), fwd),
                  pl.BlockSpec((G_CHUNK, pw), bwd_p),
                  pl.BlockSpec((G_CHUNK, G_VAL_WIDTH), fwd),
                  pl.BlockSpec((G_CHUNK, G_VAL_WIDTH), bwd),
                  pl.BlockSpec((1, 4, G_KEY_WIDTH), fwd3),
                  pl.BlockSpec((1, 4, G_KEY_WIDTH), bwd3)],
        out_specs=[pl.BlockSpec((G_CHUNK, G_VAL_WIDTH), fwd),
                   pl.BlockSpec((G_CHUNK, G_VAL_WIDTH), bwd)],
        out_shape=[jax.ShapeDtypeStruct((m, G_VAL_WIDTH), BF16),
                   jax.ShapeDtypeStruct((m, G_VAL_WIDTH), BF16)],
        scratch_shapes=[pltpu.VMEM((2 * G_HEADS, G_KEY_DIM, G_VAL_DIM), F32)],
        compiler_params=pltpu.CompilerParams(dimension_semantics=("arbitrary", "arbitrary"),
                                             vmem_limit_bytes=32 * MIB),
        name="gla_scan",
    )(p, p, v, v, ed, ed)


def _l1_epilogue_kernel(of_ref, ob_ref, g_ref, x_ref, hn_ref, wo_ref, lng_ref, lnb_ref, o_ref):
    o = of_ref[...].astype(F32) + ob_ref[...].astype(F32)
    segs = []
    for h in range(G_HEADS):
        seg = o[:, h * G_VAL_DIM:(h + 1) * G_VAL_DIM]
        ms = jnp.mean(seg * seg, axis=-1, keepdims=True)
        segs.append(seg * lax.rsqrt(ms + RMS_EPS))
    on = jnp.concatenate(segs, axis=1) * hn_ref[...]
    a = (on * _silu(g_ref[...].astype(F32))).astype(BF16)
    y = DN_ALPHA * x_ref[...] + _dot(a, wo_ref[...])
    o_ref[...] = _layer_norm(y, lng_ref[...], lnb_ref[...])


def _l1_epilogue(o_f, o_b, gate, x2d, head_norm, wo_bf16, ln_g, ln_b):
    m = x2d.shape[0]
    row = lambda i: (i, 0)
    const2 = lambda i: (0, 0)
    return pl.pallas_call(
        _l1_epilogue_kernel,
        grid=(m // EPI_ROWS,),
        in_specs=[pl.BlockSpec((EPI_ROWS, G_VAL_WIDTH), row),
                  pl.BlockSpec((EPI_ROWS, G_VAL_WIDTH), row),
                  pl.BlockSpec((EPI_ROWS, G_VAL_WIDTH), row),
                  pl.BlockSpec((EPI_ROWS, D_MODEL), row),
                  pl.BlockSpec((1, G_VAL_WIDTH), const2),
                  pl.BlockSpec((G_VAL_WIDTH, D_MODEL), const2),
                  pl.BlockSpec((1, D_MODEL), const2),
                  pl.BlockSpec((1, D_MODEL), const2)],
        out_specs=pl.BlockSpec((EPI_ROWS, D_MODEL), row),
        out_shape=jax.ShapeDtypeStruct((m, D_MODEL), F32),
        compiler_params=pltpu.CompilerParams(dimension_semantics=("arbitrary",),
                                             vmem_limit_bytes=32 * MIB),
        name="l1_epilogue",
    )(o_f, o_b, gate, x2d, head_norm, wo_bf16, ln_g, ln_b)


def _trunk(x, l0, l1):
    n_seq, seq, _ = x.shape
    assert seq % PROJ_ROWS == 0 and seq % A_BLOCK == 0 and seq % G_CHUNK == 0
    x2d = x.reshape(n_seq * seq, D_MODEL)
    q, kv, gate = _l0_proj(x2d, l0["w_in"])
    assert seq // A_BLOCK >= 2
    x1 = _l0_attn(q, kv, gate, x2d, l0["sink"], l0["bias"], l0["w_out"],
                  l0["ln_g"], l0["ln_b"], n_seq, seq // A_BLOCK)
    p, v, gate1, ed = _l1_proj(x1, l1["w_in"], l1["w_lr"], l1["w_gate"], l1["b_gate"], l1["tril"], l1["triu"])
    o_f, o_b = _gla_scan(p, v, ed, n_seq, seq // G_CHUNK)
    y = _l1_epilogue(o_f, o_b, gate1, x1, l1["head_norm"], l1["w_out"], l1["ln_g"], l1["ln_b"])
    return y.reshape(n_seq, seq, D_MODEL)


def kernel(x_prompt, x_sample, l0_w_in, l0_sink, l0_w_out, l0_ln_g, l0_ln_b,
           l1_w_in, l1_w_gate_f, l1_b_gate_f, l1_w_gate_b, l1_b_gate_b, l1_head_norm,
           l1_w_out, l1_ln_g, l1_ln_b):
    assert x_prompt.shape[1] == x_sample.shape[1]
    row = lambda t: t.reshape(1, -1).astype(F32)
    l0 = dict(w_in=l0_w_in.astype(BF16), sink=l0_sink.astype(F32), bias=_attn_bias_table(),
              w_out=l0_w_out.astype(BF16), ln_g=row(l0_ln_g), ln_b=row(l0_ln_b))
    n_main = 2 * G_KEY_WIDTH + 2 * G_VAL_WIDTH
    w_lr = jnp.zeros((D_MODEL, 128), BF16).at[:, :2 * G_RANK].set(l1_w_in[:, n_main:].astype(BF16))
    w_gate = jnp.zeros((128, 2 * G_KEY_WIDTH), BF16)
    w_gate = w_gate.at[:G_RANK, :G_KEY_WIDTH].set(l1_w_gate_f.astype(BF16))
    w_gate = w_gate.at[G_RANK:2 * G_RANK, G_KEY_WIDTH:].set(l1_w_gate_b.astype(BF16))
    b_gate = jnp.concatenate([l1_b_gate_f, l1_b_gate_b]).reshape(1, -1).astype(F32)
    tri = np.tril(np.ones((G_CHUNK, G_CHUNK), np.float32))
    l1 = dict(w_in=l1_w_in[:, :n_main].astype(BF16), w_lr=w_lr, w_gate=w_gate, b_gate=b_gate,
              tril=jnp.asarray(np.concatenate([tri, tri], axis=1), BF16),
              triu=jnp.asarray(np.concatenate([tri.T, tri.T], axis=1), BF16),
              head_norm=row(l1_head_norm), w_out=l1_w_out.astype(BF16),
              ln_g=row(l1_ln_g), ln_b=row(l1_ln_b))
    return (_trunk(x_prompt, l0, l1), _trunk(x_sample, l0, l1))
```

```python
import jax
import jax.numpy as jnp
import numpy as np
from jax import lax
from jax.experimental import pallas as pl
from jax.experimental.pallas import tpu as pltpu

F32 = jnp.float32
BF16 = jnp.bfloat16

D_MODEL = 1024
DEPTH = 2
LN_EPS = 1e-5
RMS_EPS = 1e-6
DN_ALPHA = (2 * DEPTH) ** 0.25
NEG = -1e30
LOG2E = 1.4426950408889634

A_HEADS = 16
A_KV_HEADS = 4
A_HEAD_DIM = 64
A_REP = A_HEADS // A_KV_HEADS
A_KV_WIDTH = A_KV_HEADS * A_HEAD_DIM
A_BLOCK = 128
A_PAIRS = A_HEADS // 2
A_WIN = 3 * A_BLOCK
A_QK_AHEAD = 3

G_HEADS = 4
G_KEY_DIM = 128
G_VAL_DIM = 256
G_KEY_WIDTH = G_HEADS * G_KEY_DIM
G_VAL_WIDTH = G_HEADS * G_VAL_DIM
G_RANK = 16
G_TAU = 16.0
G_CHUNK = 128
G_MID = G_CHUNK // 2

PROJ_ROWS = 512
SCAN_CHUNKS = 2
SCAN_ROWS = SCAN_CHUNKS * G_CHUNK
MIB = 1024 * 1024


def _layer_norm(y, g, b):
    mu = jnp.mean(y, axis=-1, keepdims=True)
    yc = y - mu
    var = jnp.mean(yc * yc, axis=-1, keepdims=True)
    return yc * lax.rsqrt(var + LN_EPS) * g + b


def _silu(g):
    h = 0.5 * g
    return h + h * jnp.tanh(h)


def _dot(a, b):
    return jnp.dot(a, b, preferred_element_type=F32)


def _dot_nt(a, b):
    return lax.dot_general(a, b, (((1,), (1,)), ((), ())), preferred_element_type=F32)


def _dot_tn(a, b):
    return lax.dot_general(a, b, (((0,), (0,)), ((), ())), preferred_element_type=F32)


def _l0_proj_kernel(x_ref, w_ref, q_ref, kv_ref, g_ref):
    xb = x_ref[...].astype(BF16)
    half = D_MODEL // 2
    for c in range(2):
        acc = _dot(xb, w_ref[:, c * half:(c + 1) * half])
        q_ref[:, c * half:(c + 1) * half] = (acc * (A_HEAD_DIM ** -0.5 * LOG2E)).astype(BF16)
    kv_ref[...] = _dot(xb, w_ref[:, D_MODEL:D_MODEL + 2 * A_KV_WIDTH]).astype(BF16)
    g0 = D_MODEL + 2 * A_KV_WIDTH
    for c in range(2):
        acc = _dot(xb, w_ref[:, g0 + c * half:g0 + (c + 1) * half])
        g_ref[:, c * half:(c + 1) * half] = acc.astype(BF16)


def _l0_proj(x2d, w_bf16):
    m = x2d.shape[0]
    n = w_bf16.shape[1]
    return pl.pallas_call(
        _l0_proj_kernel,
        grid=(m // PROJ_ROWS,),
        in_specs=[pl.BlockSpec((PROJ_ROWS, D_MODEL), lambda i: (i, 0)),
                  pl.BlockSpec((D_MODEL, n), lambda i: (0, 0))],
        out_specs=[pl.BlockSpec((PROJ_ROWS, D_MODEL), lambda i: (i, 0)),
                   pl.BlockSpec((PROJ_ROWS, 2 * A_KV_WIDTH), lambda i: (i, 0)),
                   pl.BlockSpec((PROJ_ROWS, D_MODEL), lambda i: (i, 0))],
        out_shape=[jax.ShapeDtypeStruct((m, D_MODEL), BF16),
                   jax.ShapeDtypeStruct((m, 2 * A_KV_WIDTH), BF16),
                   jax.ShapeDtypeStruct((m, D_MODEL), BF16)],
        compiler_params=pltpu.CompilerParams(dimension_semantics=("arbitrary",),
                                             vmem_limit_bytes=40 * MIB),
        name="l0_proj",
    )(x2d, w_bf16)


def _l0_attn_kernel(sink_ref, q_ref, kvp_ref, kvc_ref, kvn_ref, g_ref, x_ref, bias_ref,
                    wo_ref, lng_ref, lnb_ref, o_ref, a_ref):
    @pl.when((pl.program_id(0) == 0) & (pl.program_id(1) == 0))
    def _():
        a_ref[...] = jnp.zeros_like(a_ref)

    kwin = jnp.concatenate([kvp_ref[:, :A_KV_WIDTH], kvc_ref[:, :A_KV_WIDTH], kvn_ref[:, :A_KV_WIDTH]], axis=0)
    vwin = jnp.concatenate([kvp_ref[:, A_KV_WIDTH:], kvc_ref[:, A_KV_WIDTH:], kvn_ref[:, A_KV_WIDTH:]], axis=0)
    zeros = jnp.zeros((A_WIN, A_HEAD_DIM), BF16)
    lane = lax.broadcasted_iota(jnp.int32, (A_BLOCK, 2 * A_HEAD_DIM), 1)
    first_half = lane < A_HEAD_DIM
    first_row_half = lax.broadcasted_iota(jnp.int32, (1, 2 * A_HEAD_DIM), 1) < A_HEAD_DIM
    kk, vv = [], []
    for g in range(A_KV_HEADS):
        kg = kwin[:, g * A_HEAD_DIM:(g + 1) * A_HEAD_DIM]
        vg = vwin[:, g * A_HEAD_DIM:(g + 1) * A_HEAD_DIM]
        kk.append(jnp.concatenate([jnp.concatenate([kg, zeros], axis=1),
                                   jnp.concatenate([zeros, kg], axis=1)], axis=0))
        vv.append(jnp.concatenate([jnp.concatenate([vg, zeros], axis=1),
                                   jnp.concatenate([zeros, vg], axis=1)], axis=0))

    def scores(i):
        return _dot_nt(q_ref[:, i * 128:(i + 1) * 128], kk[i // (A_REP // 2)])

    def softmax(i, s):
        t = s + bias_ref[0, i]
        ps, ms, ls = [], [], []
        for hh in range(2):
            th = t[:, hh * A_WIN:(hh + 1) * A_WIN]
            m = jnp.max(th, axis=-1, keepdims=True)
            p = jnp.exp2(th - m)
            ms.append(m)
            ls.append(jnp.sum(p, axis=-1, keepdims=True))
            ps.append(p.astype(BF16))
        sink = jnp.where(first_row_half, sink_ref[2 * i] * LOG2E, sink_ref[2 * i + 1] * LOG2E)
        row_max = jnp.where(first_half, ms[0], ms[1])
        row_sum = jnp.where(first_half, ls[0], ls[1])
        return jnp.concatenate(ps, axis=1), 1.0 / (row_sum + jnp.exp2(sink - row_max))

    def weighted_values(i, p, r):
        return _dot(p, vv[i // (A_REP // 2)]) * r

    s_ready = {i: scores(i) for i in range(min(A_QK_AHEAD, A_PAIRS))}
    y = DN_ALPHA * x_ref[...] + _dot(a_ref[...], wo_ref[...])
    o_ref[...] = _layer_norm(y, lng_ref[...], lnb_ref[...])
    outs, pending = [], None
    for i in range(A_PAIRS):
        if i + A_QK_AHEAD < A_PAIRS:
            s_ready[i + A_QK_AHEAD] = scores(i + A_QK_AHEAD)
        p, r = softmax(i, s_ready.pop(i))
        if pending is not None:
            outs.append(weighted_values(*pending))
        pending = (i, p, r)
    outs.append(weighted_values(*pending))
    o = jnp.concatenate(outs, axis=1)
    a_ref[...] = (o * _silu(g_ref[...].astype(F32))).astype(BF16)


def _l0_attn(q, kv, gate, x2d, sink, bias, wo_bf16, ln_g, ln_b, n_seq, n_blk):
    m = x2d.shape[0]
    blk = lambda n: jnp.minimum(n, n_blk - 1)
    row = lambda b, n: (b * n_blk + blk(n), 0)
    prev = lambda b, n: (b * n_blk + jnp.maximum(blk(n) - 1, 0), 0)
    nxt = lambda b, n: (b * n_blk + jnp.minimum(blk(n) + 1, n_blk - 1), 0)
    done = lambda b, n: (b * n_blk + jnp.maximum(n - 1, 0), 0)
    const2 = lambda b, n: (0, 0)
    variant = lambda b, n: (jnp.where(n == 0, 0, jnp.where(n >= n_blk - 1, 2, 1)), 0, 0, 0)
    return pl.pallas_call(
        _l0_attn_kernel,
        grid=(n_seq, n_blk + 1),
        in_specs=[pl.BlockSpec(memory_space=pltpu.SMEM),
                  pl.BlockSpec((A_BLOCK, D_MODEL), row),
                  pl.BlockSpec((A_BLOCK, 2 * A_KV_WIDTH), prev),
                  pl.BlockSpec((A_BLOCK, 2 * A_KV_WIDTH), row),
                  pl.BlockSpec((A_BLOCK, 2 * A_KV_WIDTH), nxt),
                  pl.BlockSpec((A_BLOCK, D_MODEL), row),
                  pl.BlockSpec((A_BLOCK, D_MODEL), done),
                  pl.BlockSpec((1, A_PAIRS, A_BLOCK, 2 * A_WIN), variant),
                  pl.BlockSpec((D_MODEL, D_MODEL), const2),
                  pl.BlockSpec((1, D_MODEL), const2),
                  pl.BlockSpec((1, D_MODEL), const2)],
        out_specs=pl.BlockSpec((A_BLOCK, D_MODEL), done),
        out_shape=jax.ShapeDtypeStruct((m, D_MODEL), F32),
        scratch_shapes=[pltpu.VMEM((A_BLOCK, D_MODEL), BF16)],
        compiler_params=pltpu.CompilerParams(dimension_semantics=("arbitrary", "arbitrary"),
                                             vmem_limit_bytes=40 * MIB),
        name="l0_attn",
    )(sink, q, kv, kv, kv, gate, x2d, bias, wo_bf16, ln_g, ln_b)


def _attn_bias_table():
    qi = np.arange(A_BLOCK)[:, None] + A_BLOCK
    kj = np.arange(A_WIN)[None, :]
    dist = np.abs(qi - kj).astype(np.float32)
    in_win = dist <= A_BLOCK
    slopes = (2.0 ** (-8.0 * np.arange(1, A_HEADS + 1) / A_HEADS)).astype(np.float32)
    per_head = (-slopes[:, None, None] * dist[None]).astype(np.float32) * np.float32(LOG2E)
    variants = []
    for valid in (kj >= A_BLOCK, kj >= 0, kj < 2 * A_BLOCK):
        b = np.where((in_win & valid)[None], per_head, np.float32(NEG)).astype(np.float32)
        variants.append(b.reshape(A_PAIRS, 2, A_BLOCK, A_WIN).transpose(0, 2, 1, 3).reshape(A_PAIRS, A_BLOCK, 2 * A_WIN))
    return jnp.asarray(np.stack(variants))


def _log2_decay(z):
    return (jnp.minimum(z, 0.0) - jnp.log(1.0 + jnp.exp(-jnp.abs(z)))) * (LOG2E / G_TAU)


def _split_cumsum(tri2_bf16, la):
    hi = la.astype(BF16)
    lo = (la - hi.astype(F32)).astype(BF16)
    return _dot(tri2_bf16, jnp.concatenate([hi, lo], axis=0))


def _l1_proj_kernel(x_ref, w_ref, wlr_ref, wg_ref, bg_ref, tril_ref, triu_ref,
                    p_ref, v_ref, g_ref, ed_ref):
    xb = x_ref[...].astype(BF16)
    kw = G_KEY_WIDTH
    lr = _dot(xb, wlr_ref[...]).astype(BF16)
    z = _dot(lr, wg_ref[...]) + bg_ref[...]
    q = _dot(xb, w_ref[:, 0:kw]) * (G_KEY_DIM ** -0.5)
    k = _dot(xb, w_ref[:, kw:2 * kw])
    la = _log2_decay(z)
    n_chunks = PROJ_ROWS // G_CHUNK
    cums = [(_split_cumsum(tril_ref[...], la[c * G_CHUNK:(c + 1) * G_CHUNK, 0:kw]),
             _split_cumsum(triu_ref[...], la[c * G_CHUNK:(c + 1) * G_CHUNK, kw:2 * kw]))
            for c in range(n_chunks)]
    for c in range(2):
        c0 = 2 * kw + c * kw
        v_ref[:, c * kw:(c + 1) * kw] = _dot(xb, w_ref[:, c0:c0 + kw]).astype(BF16)
    for c in range(n_chunks):
        r0 = c * G_CHUNK
        qc = q[r0:r0 + G_CHUNK]
        kc = k[r0:r0 + G_CHUNK]
        b, s = cums[c]
        b_mid = b[G_MID - 1:G_MID]
        b_end = b[G_CHUNK - 1:G_CHUNK]
        p_ref[r0:r0 + G_CHUNK, 0:kw] = (qc * jnp.exp2(b - b_mid)).astype(BF16)
        p_ref[r0:r0 + G_CHUNK, kw:2 * kw] = (kc * jnp.exp2(b_mid - b)).astype(BF16)
        p_ref[r0:r0 + G_CHUNK, 2 * kw:3 * kw] = (kc * jnp.exp2(b_end - b)).astype(BF16)
        ed_ref[c, 0:1, :] = jnp.exp2(b_mid)
        ed_ref[c, 1:2, :] = jnp.exp2(b_end)
        s_mid = s[G_MID:G_MID + 1]
        s_end = s[0:1]
        p_ref[r0:r0 + G_CHUNK, 3 * kw:4 * kw] = (qc * jnp.exp2(s - s_mid)).astype(BF16)
        p_ref[r0:r0 + G_CHUNK, 4 * kw:5 * kw] = (kc * jnp.exp2(s_mid - s)).astype(BF16)
        p_ref[r0:r0 + G_CHUNK, 5 * kw:6 * kw] = (kc * jnp.exp2(s_end - s)).astype(BF16)
        ed_ref[c, 2:3, :] = jnp.exp2(s_mid)
        ed_ref[c, 3:4, :] = jnp.exp2(s_end)
    for c in range(2):
        c0 = 2 * kw + G_VAL_WIDTH + c * kw
        g_ref[:, c * kw:(c + 1) * kw] = _dot(xb, w_ref[:, c0:c0 + kw]).astype(BF16)


def _l1_proj(x2d, w_bf16, wlr, wg, bg, tril, triu):
    m = x2d.shape[0]
    n_main = w_bf16.shape[1]
    cpt = PROJ_ROWS // G_CHUNK
    const2 = lambda i: (0, 0)
    return pl.pallas_call(
        _l1_proj_kernel,
        grid=(m // PROJ_ROWS,),
        in_specs=[pl.BlockSpec((PROJ_ROWS, D_MODEL), lambda i: (i, 0)),
                  pl.BlockSpec((D_MODEL, n_main), const2),
                  pl.BlockSpec((D_MODEL, 128), const2),
                  pl.BlockSpec((128, 2 * G_KEY_WIDTH), const2),
                  pl.BlockSpec((1, 2 * G_KEY_WIDTH), const2),
                  pl.BlockSpec((G_CHUNK, 2 * G_CHUNK), const2),
                  pl.BlockSpec((G_CHUNK, 2 * G_CHUNK), const2)],
        out_specs=[pl.BlockSpec((PROJ_ROWS, 6 * G_KEY_WIDTH), lambda i: (i, 0)),
                   pl.BlockSpec((PROJ_ROWS, G_VAL_WIDTH), lambda i: (i, 0)),
                   pl.BlockSpec((PROJ_ROWS, G_VAL_WIDTH), lambda i: (i, 0)),
                   pl.BlockSpec((cpt, 4, G_KEY_WIDTH), lambda i: (i, 0, 0))],
        out_shape=[jax.ShapeDtypeStruct((m, 6 * G_KEY_WIDTH), BF16),
                   jax.ShapeDtypeStruct((m, G_VAL_WIDTH), BF16),
                   jax.ShapeDtypeStruct((m, G_VAL_WIDTH), BF16),
                   jax.ShapeDtypeStruct((m // G_CHUNK, 4, G_KEY_WIDTH), F32)],
        compiler_params=pltpu.CompilerParams(dimension_semantics=("arbitrary",),
                                             vmem_limit_bytes=56 * MIB),
        name="l1_proj",
    )(x2d, w_bf16, wlr, wg, bg, tril, triu)


def _gla_direction(p_ref, v_ref, ed_ref, state_ref, backward):
    rows = lax.broadcasted_iota(jnp.int32, (G_CHUNK, G_CHUNK), 0)
    cols = lax.broadcasted_iota(jnp.int32, (G_CHUNK, G_CHUNK), 1)
    eye = (rows == cols).astype(F32)
    mask = rows < cols if backward else rows >= cols
    d = 1 if backward else 0
    order = list(range(SCAN_CHUNKS))[::-1] if backward else list(range(SCAN_CHUNKS))
    kw = G_KEY_WIDTH
    heads = range(G_HEADS)
    rsl = lambda c: slice(c * G_CHUNK, (c + 1) * G_CHUNK)
    ksl = lambda part, h: slice(part * kw + h * G_KEY_DIM, part * kw + (h + 1) * G_KEY_DIM)
    vsl = lambda h: slice(h * G_VAL_DIM, (h + 1) * G_VAL_DIM)
    atts = {(c, h): jnp.where(mask, _dot_nt(p_ref[rsl(c), ksl(0, h)], p_ref[rsl(c), ksl(1, h)]), 0.0).astype(BF16)
            for c in order for h in heads}
    states = [state_ref[h] for h in heads]
    outs = {}
    for c in order:
        inters = []
        for h in heads:
            e_row = ed_ref[c, 2 * d:2 * d + 1, h * G_KEY_DIM:(h + 1) * G_KEY_DIM]
            d_row = ed_ref[c, 2 * d + 1:2 * d + 2, h * G_KEY_DIM:(h + 1) * G_KEY_DIM]
            e_col = jnp.sum(eye * e_row, axis=1, keepdims=True)
            d_col = jnp.sum(eye * d_row, axis=1, keepdims=True)
            inters.append(_dot(p_ref[rsl(c), ksl(0, h)], (states[h] * e_col).astype(BF16)))
            states[h] = states[h] * d_col + _dot_tn(p_ref[rsl(c), ksl(2, h)], v_ref[rsl(c), vsl(h)])
        for h in heads:
            outs[c, h] = _dot(atts[c, h], v_ref[rsl(c), vsl(h)]) + inters[h]
    for h in heads:
        state_ref[h] = states[h]
    return outs


def _gla_bwd_kernel(p_ref, v_ref, ed_ref, o_ref, state_ref):
    @pl.when(pl.program_id(1) == 0)
    def _():
        state_ref[...] = jnp.zeros_like(state_ref)

    outs = _gla_direction(p_ref, v_ref, ed_ref, state_ref, backward=True)
    for (c, h), o in outs.items():
        o_ref[c * G_CHUNK:(c + 1) * G_CHUNK, h * G_VAL_DIM:(h + 1) * G_VAL_DIM] = o.astype(BF16)


def _gla_bwd(p, v, ed, n_seq, n_step):
    m = v.shape[0]
    rev = lambda b, j: (b * n_step + n_step - 1 - j, 0)
    rev_p = lambda b, j: (b * n_step + n_step - 1 - j, 1)
    rev3 = lambda b, j: (b * n_step + n_step - 1 - j, 0, 0)
    return pl.pallas_call(
        _gla_bwd_kernel,
        grid=(n_seq, n_step),
        in_specs=[pl.BlockSpec((SCAN_ROWS, 3 * G_KEY_WIDTH), rev_p),
                  pl.BlockSpec((SCAN_ROWS, G_VAL_WIDTH), rev),
                  pl.BlockSpec((SCAN_CHUNKS, 4, G_KEY_WIDTH), rev3)],
        out_specs=pl.BlockSpec((SCAN_ROWS, G_VAL_WIDTH), rev),
        out_shape=jax.ShapeDtypeStruct((m, G_VAL_WIDTH), BF16),
        scratch_shapes=[pltpu.VMEM((G_HEADS, G_KEY_DIM, G_VAL_DIM), F32)],
        compiler_params=pltpu.CompilerParams(dimension_semantics=("arbitrary", "arbitrary"),
                                             vmem_limit_bytes=32 * MIB),
        name="gla_bwd",
    )(p, v, ed)


def _gla_fwd_kernel(p_ref, v_ref, ed_ref, ob_ref, g_ref, x_ref, hn_ref, wo_ref, lng_ref, lnb_ref,
                    o_ref, state_ref):
    @pl.when(pl.program_id(1) == 0)
    def _():
        state_ref[...] = jnp.zeros_like(state_ref)

    outs = _gla_direction(p_ref, v_ref, ed_ref, state_ref, backward=False)
    acts = []
    for c in range(SCAN_CHUNKS):
        rsl = slice(c * G_CHUNK, (c + 1) * G_CHUNK)
        segs = []
        for h in range(G_HEADS):
            seg = outs[c, h] + ob_ref[rsl, h * G_VAL_DIM:(h + 1) * G_VAL_DIM].astype(F32)
            ms = jnp.mean(seg * seg, axis=-1, keepdims=True)
            segs.append(seg * lax.rsqrt(ms + RMS_EPS))
        on = jnp.concatenate(segs, axis=1) * hn_ref[...]
        acts.append((on * _silu(g_ref[rsl, :].astype(F32))).astype(BF16))
    y = DN_ALPHA * x_ref[...] + _dot(jnp.concatenate(acts, axis=0), wo_ref[...])
    o_ref[...] = _layer_norm(y, lng_ref[...], lnb_ref[...])


def _gla_fwd(p, v, ed, o_b, gate, x2d, head_norm, wo_bf16, ln_g, ln_b, n_seq, n_step):
    m = x2d.shape[0]
    row = lambda b, j: (b * n_step + j, 0)
    row3 = lambda b, j: (b * n_step + j, 0, 0)
    const2 = lambda b, j: (0, 0)
    return pl.pallas_call(
        _gla_fwd_kernel,
        grid=(n_seq, n_step),
        in_specs=[pl.BlockSpec((SCAN_ROWS, 3 * G_KEY_WIDTH), row),
                  pl.BlockSpec((SCAN_ROWS, G_VAL_WIDTH), row),
                  pl.BlockSpec((SCAN_CHUNKS, 4, G_KEY_WIDTH), row3),
                  pl.BlockSpec((SCAN_ROWS, G_VAL_WIDTH), row),
                  pl.BlockSpec((SCAN_ROWS, G_VAL_WIDTH), row),
                  pl.BlockSpec((SCAN_ROWS, D_MODEL), row),
                  pl.BlockSpec((1, G_VAL_WIDTH), const2),
                  pl.BlockSpec((G_VAL_WIDTH, D_MODEL), const2),
                  pl.BlockSpec((1, D_MODEL), const2),
                  pl.BlockSpec((1, D_MODEL), const2)],
        out_specs=pl.BlockSpec((SCAN_ROWS, D_MODEL), row),
        out_shape=jax.ShapeDtypeStruct((m, D_MODEL), F32),
        scratch_shapes=[pltpu.VMEM((G_HEADS, G_KEY_DIM, G_VAL_DIM), F32)],
        compiler_params=pltpu.CompilerParams(dimension_semantics=("arbitrary", "arbitrary"),
                                             vmem_limit_bytes=40 * MIB),
        name="gla_fwd",
    )(p, v, ed, o_b, gate, x2d, head_norm, wo_bf16, ln_g, ln_b)


def _trunk(x, l0, l1):
    n_seq, seq, _ = x.shape
    assert seq % PROJ_ROWS == 0 and seq % A_BLOCK == 0 and seq % G_CHUNK == 0
    x2d = x.reshape(n_seq * seq, D_MODEL)
    q, kv, gate = _l0_proj(x2d, l0["w_in"])
    assert seq // A_BLOCK >= 2
    x1 = _l0_attn(q, kv, gate, x2d, l0["sink"], l0["bias"], l0["w_out"],
                  l0["ln_g"], l0["ln_b"], n_seq, seq // A_BLOCK)
    p, v, gate1, ed = _l1_proj(x1, l1["w_in"], l1["w_lr"], l1["w_gate"], l1["b_gate"], l1["tril"], l1["triu"])
    o_b = _gla_bwd(p, v, ed, n_seq, seq // SCAN_ROWS)
    y = _gla_fwd(p, v, ed, o_b, gate1, x1, l1["head_norm"], l1["w_out"], l1["ln_g"], l1["ln_b"],
                 n_seq, seq // SCAN_ROWS)
    return y.reshape(n_seq, seq, D_MODEL)


def kernel(x_prompt, x_sample, l0_w_in, l0_sink, l0_w_out, l0_ln_g, l0_ln_b,
           l1_w_in, l1_w_gate_f, l1_b_gate_f, l1_w_gate_b, l1_b_gate_b, l1_head_norm,
           l1_w_out, l1_ln_g, l1_ln_b):
    assert x_prompt.shape[1] == x_sample.shape[1]
    row = lambda t: t.reshape(1, -1).astype(F32)
    l0 = dict(w_in=l0_w_in.astype(BF16), sink=l0_sink.astype(F32), bias=_attn_bias_table(),
              w_out=l0_w_out.astype(BF16), ln_g=row(l0_ln_g), ln_b=row(l0_ln_b))
    n_main = 2 * G_KEY_WIDTH + 2 * G_VAL_WIDTH
    w_lr = jnp.zeros((D_MODEL, 128), BF16).at[:, :2 * G_RANK].set(l1_w_in[:, n_main:].astype(BF16))
    w_gate = jnp.zeros((128, 2 * G_KEY_WIDTH), BF16)
    w_gate = w_gate.at[:G_RANK, :G_KEY_WIDTH].set(l1_w_gate_f.astype(BF16))
    w_gate = w_gate.at[G_RANK:2 * G_RANK, G_KEY_WIDTH:].set(l1_w_gate_b.astype(BF16))
    b_gate = jnp.concatenate([l1_b_gate_f, l1_b_gate_b]).reshape(1, -1).astype(F32)
    tri = np.tril(np.ones((G_CHUNK, G_CHUNK), np.float32))
    l1 = dict(w_in=l1_w_in[:, :n_main].astype(BF16), w_lr=w_lr, w_gate=w_gate, b_gate=b_gate,
              tril=jnp.asarray(np.concatenate([tri, tri], axis=1), BF16),
              triu=jnp.asarray(np.concatenate([tri.T, tri.T], axis=1), BF16),
              head_norm=row(l1_head_norm), w_out=l1_w_out.astype(BF16),
              ln_g=row(l1_ln_g), ln_b=row(l1_ln_b))
    return (_trunk(x_prompt, l0, l1), _trunk(x_sample, l0, l1))
```

```python
import jax
import jax.numpy as jnp
import numpy as np
from jax import lax
from jax.experimental import pallas as pl
from jax.experimental.pallas import tpu as pltpu

F32 = jnp.float32
BF16 = jnp.bfloat16

D_MODEL = 1024
DEPTH = 2
LN_EPS = 1e-5
RMS_EPS = 1e-6
DN_ALPHA = (2 * DEPTH) ** 0.25
NEG = -1e30
LOG2E = 1.4426950408889634

A_HEADS = 16
A_KV_HEADS = 4
A_HEAD_DIM = 64
A_REP = A_HEADS // A_KV_HEADS
A_KV_WIDTH = A_KV_HEADS * A_HEAD_DIM
A_BLOCK = 128
A_PAIRS = A_HEADS // 2
A_WIN = 3 * A_BLOCK
A_QK_AHEAD = 3

G_HEADS = 4
G_KEY_DIM = 128
G_VAL_DIM = 256
G_KEY_WIDTH = G_HEADS * G_KEY_DIM
G_VAL_WIDTH = G_HEADS * G_VAL_DIM
G_RANK = 16
G_TAU = 16.0
G_CHUNK = 128
G_MID = G_CHUNK // 2

L0_SUB = 2
L0_ROWS = L0_SUB * A_BLOCK
PROJ_ROWS = 512
SCAN_CHUNKS = 2
SCAN_ROWS = SCAN_CHUNKS * G_CHUNK
MIB = 1024 * 1024


def _layer_norm(y, g, b):
    mu = jnp.mean(y, axis=-1, keepdims=True)
    yc = y - mu
    var = jnp.mean(yc * yc, axis=-1, keepdims=True)
    return yc * lax.rsqrt(var + LN_EPS) * g + b


def _silu(g):
    h = 0.5 * g
    return h + h * jnp.tanh(h)


def _dot(a, b):
    return jnp.dot(a, b, preferred_element_type=F32)


def _dot_nt(a, b):
    return lax.dot_general(a, b, (((1,), (1,)), ((), ())), preferred_element_type=F32)


def _dot_tn(a, b):
    return lax.dot_general(a, b, (((0,), (0,)), ((), ())), preferred_element_type=F32)


def _l0_kernel(sink_ref, x_ref, xo_ref, w_ref, bias_a_ref, bias_b_ref, wo_ref, lng_ref, lnb_ref, wlr_ref,
               o_ref, lr_ref, q_s, g_s, kv_s, a_s):
    t = pl.program_id(0)

    @pl.when(t == 0)
    def _():
        q_s[...] = jnp.zeros_like(q_s)
        g_s[...] = jnp.zeros_like(g_s)
        kv_s[...] = jnp.zeros_like(kv_s)
        a_s[...] = jnp.zeros_like(a_s)

    new2, old2 = lax.rem(t, 2), lax.rem(t + 1, 2)
    k_new, k_cur, k_old = lax.rem(t, 3), lax.rem(t + 2, 3), lax.rem(t + 1, 3)
    q_blk = q_s[old2]
    g_blk = g_s[old2]
    kv_cur = kv_s[k_cur]
    kv_old_tail = kv_s[k_old, A_BLOCK:, :]
    a_prev = a_s[...]
    xb = x_ref[...].astype(BF16)

    zeros = jnp.zeros((A_WIN, A_HEAD_DIM), BF16)
    lane = lax.broadcasted_iota(jnp.int32, (A_BLOCK, 2 * A_HEAD_DIM), 1)
    first_half = lane < A_HEAD_DIM
    first_row_half = lax.broadcasted_iota(jnp.int32, (1, 2 * A_HEAD_DIM), 1) < A_HEAD_DIM

    ones_row = lax.broadcasted_iota(jnp.int32, (2 * A_WIN, 2 * A_HEAD_DIM), 0) < A_WIN
    ones_lane = lax.broadcasted_iota(jnp.int32, (2 * A_WIN, 2 * A_HEAD_DIM), 1) < A_HEAD_DIM
    ones_blk = (ones_row == ones_lane).astype(BF16)

    def slabs(prev, cur, nxt):
        win = jnp.concatenate([prev, cur, nxt], axis=0)
        kk, vv = [], []
        for g in range(A_KV_HEADS):
            kg = win[:, g * A_HEAD_DIM:(g + 1) * A_HEAD_DIM]
            vg = win[:, A_KV_WIDTH + g * A_HEAD_DIM:A_KV_WIDTH + (g + 1) * A_HEAD_DIM]
            kk.append(jnp.concatenate([jnp.concatenate([kg, zeros], axis=1),
                                       jnp.concatenate([zeros, kg], axis=1)], axis=0))
            vv.append(jnp.concatenate([jnp.concatenate([jnp.concatenate([vg, zeros], axis=1),
                                                        jnp.concatenate([zeros, vg], axis=1)], axis=0),
                                       ones_blk], axis=1))
        return kk, vv

    bias_refs = (bias_a_ref, bias_b_ref)
    win_slabs = [slabs(kv_old_tail, kv_cur[:A_BLOCK], kv_cur[A_BLOCK:]), None]
    n_stage = L0_SUB * A_PAIRS

    def scores(j):
        sub, i = divmod(j, A_PAIRS)
        qp = q_blk[sub * A_BLOCK:(sub + 1) * A_BLOCK, i * 128:(i + 1) * 128]
        return _dot_nt(qp, win_slabs[sub][0][i // (A_REP // 2)])

    def softmax(j, s):
        sub, i = divmod(j, A_PAIRS)
        t_ = s + bias_refs[sub][0, i]
        ps, ms = [], []
        for hh in range(2):
            th = t_[:, hh * A_WIN:(hh + 1) * A_WIN]
            m = jnp.max(th, axis=-1, keepdims=True)
            ps.append(jnp.exp2(th - m).astype(BF16))
            ms.append(m)
        sink = jnp.where(first_row_half, sink_ref[2 * i] * LOG2E, sink_ref[2 * i + 1] * LOG2E)
        return jnp.concatenate(ps, axis=1), jnp.exp2(sink - jnp.where(first_half, ms[0], ms[1]))

    def weighted_values(j, p, sink_term):
        sub, i = divmod(j, A_PAIRS)
        rows = slice(sub * A_BLOCK, (sub + 1) * A_BLOCK)
        pv = _dot(p, win_slabs[sub][1][i // (A_REP // 2)])
        o = pv[:, :128] * (1.0 / (pv[:, 128:] + sink_term))
        a_s[rows, i * 128:(i + 1) * 128] = (o * _silu(g_blk[rows, i * 128:(i + 1) * 128].astype(F32))).astype(BF16)

    y_chunks = []
    fw = D_MODEL // 4

    def out_proj_chunk(c):
        y_chunks.append(_dot(a_prev, wo_ref[:, c * fw:(c + 1) * fw]))
        if c == 3:
            y = DN_ALPHA * xo_ref[...] + jnp.concatenate(y_chunks, axis=1)
            y_chunks[:] = [_layer_norm(y, lng_ref[...], lnb_ref[...])]
            o_ref[...] = y_chunks[0]

    def low_rank_chunk(c):
        lr_ref[...] = _dot(y_chunks[0].astype(BF16), wlr_ref[...]).astype(BF16)

    def q_chunk(c):
        acc = _dot(xb, w_ref[:, c * fw:(c + 1) * fw])
        q_s[new2, :, c * fw:(c + 1) * fw] = (acc * (A_HEAD_DIM ** -0.5 * LOG2E)).astype(BF16)

    def gate_chunk(c):
        c0 = D_MODEL + 2 * A_KV_WIDTH + c * fw
        g_s[new2, :, c * fw:(c + 1) * fw] = _dot(xb, w_ref[:, c0:c0 + fw]).astype(BF16)

    kv_parts = []

    def kv_chunk(c):
        c0 = D_MODEL + c * A_KV_WIDTH
        kv_parts.append(_dot(xb, w_ref[:, c0:c0 + A_KV_WIDTH]).astype(BF16))
        if c == 1:
            kv_new = jnp.concatenate(kv_parts, axis=1)
            kv_s[k_new] = kv_new
            win_slabs[1] = slabs(kv_cur[:A_BLOCK], kv_cur[A_BLOCK:], kv_new[:A_BLOCK])

    fillers = [(kv_chunk, c) for c in range(2)] + [(out_proj_chunk, c) for c in range(4)]
    assert len(fillers) <= n_stage and A_QK_AHEAD + 2 <= A_PAIRS

    fillers += ([(q_chunk, c) for c in range(3)] + [(low_rank_chunk, 0), (q_chunk, 3)]
                + [(gate_chunk, c) for c in range(4)])
    assert len(fillers) <= n_stage

    s_ready = {j: scores(j) for j in range(A_QK_AHEAD)}
    pending = None
    for j in range(n_stage):
        if j + A_QK_AHEAD < n_stage:
            s_ready[j + A_QK_AHEAD] = scores(j + A_QK_AHEAD)
        p, sink_term = softmax(j, s_ready.pop(j))
        if pending is not None:
            weighted_values(*pending)
        pending = (j, p, sink_term)
        if j < len(fillers):
            fn, c = fillers[j]
            fn(c)
    weighted_values(*pending)


def _l0_layer(x2d, sink, w_bf16, bias, wo_bf16, ln_g, ln_b, w_lr, blocks_per_seq):
    m = x2d.shape[0]
    done = lambda t: (jnp.maximum(t - 2, 0), 0)
    n_blk = m // L0_ROWS
    n_w = w_bf16.shape[1]
    const2 = lambda t: (0, 0)
    var_a = lambda t: (jnp.where(lax.rem(t + blocks_per_seq - 1, blocks_per_seq) == 0, 0, 1), 0, 0, 0)
    var_b = lambda t: (jnp.where(lax.rem(t, blocks_per_seq) == 0, 2, 1), 0, 0, 0)
    return pl.pallas_call(
        _l0_kernel,
        grid=(n_blk + 2,),
        in_specs=[pl.BlockSpec(memory_space=pltpu.SMEM),
                  pl.BlockSpec((L0_ROWS, D_MODEL), lambda t: (jnp.minimum(t, n_blk - 1), 0)),
                  pl.BlockSpec((L0_ROWS, D_MODEL), done),
                  pl.BlockSpec((D_MODEL, n_w), const2),
                  pl.BlockSpec((1, A_PAIRS, A_BLOCK, 2 * A_WIN), var_a),
                  pl.BlockSpec((1, A_PAIRS, A_BLOCK, 2 * A_WIN), var_b),
                  pl.BlockSpec((D_MODEL, D_MODEL), const2),
                  pl.BlockSpec((1, D_MODEL), const2),
                  pl.BlockSpec((1, D_MODEL), const2),
                  pl.BlockSpec((D_MODEL, 128), const2)],
        out_specs=[pl.BlockSpec((L0_ROWS, D_MODEL), done),
                   pl.BlockSpec((L0_ROWS, 128), done)],
        out_shape=[jax.ShapeDtypeStruct((m, D_MODEL), F32),
                   jax.ShapeDtypeStruct((m, 128), BF16)],
        scratch_shapes=[pltpu.VMEM((2, L0_ROWS, D_MODEL), BF16),
                        pltpu.VMEM((2, L0_ROWS, D_MODEL), BF16),
                        pltpu.VMEM((3, L0_ROWS, 2 * A_KV_WIDTH), BF16),
                        pltpu.VMEM((L0_ROWS, D_MODEL), BF16)],
        compiler_params=pltpu.CompilerParams(dimension_semantics=("arbitrary",),
                                             vmem_limit_bytes=56 * MIB),
        name="l0_layer",
    )(sink, x2d, x2d, w_bf16, bias, bias, wo_bf16, ln_g, ln_b, w_lr)


def _attn_bias_table():
    qi = np.arange(A_BLOCK)[:, None] + A_BLOCK
    kj = np.arange(A_WIN)[None, :]
    dist = np.abs(qi - kj).astype(np.float32)
    in_win = dist <= A_BLOCK
    slopes = (2.0 ** (-8.0 * np.arange(1, A_HEADS + 1) / A_HEADS)).astype(np.float32)
    per_head = (-slopes[:, None, None] * dist[None]).astype(np.float32) * np.float32(LOG2E)
    variants = []
    for valid in (kj >= A_BLOCK, kj >= 0, kj < 2 * A_BLOCK):
        b = np.where((in_win & valid)[None], per_head, np.float32(NEG)).astype(np.float32)
        variants.append(b.reshape(A_PAIRS, 2, A_BLOCK, A_WIN).transpose(0, 2, 1, 3).reshape(A_PAIRS, A_BLOCK, 2 * A_WIN))
    return jnp.asarray(np.stack(variants))


def _log2_decay(z):
    return (jnp.minimum(z, 0.0) - jnp.log(1.0 + jnp.exp(-jnp.abs(z)))) * (LOG2E / G_TAU)


def _split_cumsum(tri2_bf16, la):
    hi = la.astype(BF16)
    lo = (la - hi.astype(F32)).astype(BF16)
    return _dot(tri2_bf16, jnp.concatenate([hi, lo], axis=0))


def _l1_proj_kernel(x_ref, lr_ref, w_ref, wg_ref, bg_ref, tril_ref, triu_ref,
                    p_ref, v_ref, g_ref, ed_ref):
    xb = x_ref[...].astype(BF16)
    kw = G_KEY_WIDTH
    z = _dot(lr_ref[...], wg_ref[...]) + bg_ref[...]
    q = _dot(xb, w_ref[:, 0:kw]) * (G_KEY_DIM ** -0.5)
    k = _dot(xb, w_ref[:, kw:2 * kw])
    la = _log2_decay(z)
    n_chunks = PROJ_ROWS // G_CHUNK
    cums = [(_split_cumsum(tril_ref[...], la[c * G_CHUNK:(c + 1) * G_CHUNK, 0:kw]),
             _split_cumsum(triu_ref[...], la[c * G_CHUNK:(c + 1) * G_CHUNK, kw:2 * kw]))
            for c in range(n_chunks)]
    for c in range(2):
        c0 = 2 * kw + c * kw
        v_ref[:, c * kw:(c + 1) * kw] = _dot(xb, w_ref[:, c0:c0 + kw]).astype(BF16)
    for c in range(n_chunks):
        r0 = c * G_CHUNK
        qc = q[r0:r0 + G_CHUNK]
        kc = k[r0:r0 + G_CHUNK]
        b, s = cums[c]
        b_mid = b[G_MID - 1:G_MID]
        b_end = b[G_CHUNK - 1:G_CHUNK]
        p_ref[r0:r0 + G_CHUNK, 0:kw] = (qc * jnp.exp2(b - b_mid)).astype(BF16)
        p_ref[r0:r0 + G_CHUNK, kw:2 * kw] = (kc * jnp.exp2(b_mid - b)).astype(BF16)
        p_ref[r0:r0 + G_CHUNK, 2 * kw:3 * kw] = (kc * jnp.exp2(b_end - b)).astype(BF16)
        ed_ref[c, 0:1, :] = jnp.exp2(b_mid)
        ed_ref[c, 1:2, :] = jnp.exp2(b_end)
        s_mid = s[G_MID:G_MID + 1]
        s_end = s[0:1]
        p_ref[r0:r0 + G_CHUNK, 3 * kw:4 * kw] = (qc * jnp.exp2(s - s_mid)).astype(BF16)
        p_ref[r0:r0 + G_CHUNK, 4 * kw:5 * kw] = (kc * jnp.exp2(s_mid - s)).astype(BF16)
        p_ref[r0:r0 + G_CHUNK, 5 * kw:6 * kw] = (kc * jnp.exp2(s_end - s)).astype(BF16)
        ed_ref[c, 2:3, :] = jnp.exp2(s_mid)
        ed_ref[c, 3:4, :] = jnp.exp2(s_end)
    for c in range(2):
        c0 = 2 * kw + G_VAL_WIDTH + c * kw
        g_ref[:, c * kw:(c + 1) * kw] = _dot(xb, w_ref[:, c0:c0 + kw]).astype(BF16)


def _l1_proj(x2d, lr, w_bf16, wg, bg, tril, triu):
    m = x2d.shape[0]
    n_main = w_bf16.shape[1]
    cpt = PROJ_ROWS // G_CHUNK
    const2 = lambda i: (0, 0)
    return pl.pallas_call(
        _l1_proj_kernel,
        grid=(m // PROJ_ROWS,),
        in_specs=[pl.BlockSpec((PROJ_ROWS, D_MODEL), lambda i: (i, 0)),
                  pl.BlockSpec((PROJ_ROWS, 128), lambda i: (i, 0)),
                  pl.BlockSpec((D_MODEL, n_main), const2),
                  pl.BlockSpec((128, 2 * G_KEY_WIDTH), const2),
                  pl.BlockSpec((1, 2 * G_KEY_WIDTH), const2),
                  pl.BlockSpec((G_CHUNK, 2 * G_CHUNK), const2),
                  pl.BlockSpec((G_CHUNK, 2 * G_CHUNK), const2)],
        out_specs=[pl.BlockSpec((PROJ_ROWS, 6 * G_KEY_WIDTH), lambda i: (i, 0)),
                   pl.BlockSpec((PROJ_ROWS, G_VAL_WIDTH), lambda i: (i, 0)),
                   pl.BlockSpec((PROJ_ROWS, G_VAL_WIDTH), lambda i: (i, 0)),
                   pl.BlockSpec((cpt, 4, G_KEY_WIDTH), lambda i: (i, 0, 0))],
        out_shape=[jax.ShapeDtypeStruct((m, 6 * G_KEY_WIDTH), BF16),
                   jax.ShapeDtypeStruct((m, G_VAL_WIDTH), BF16),
                   jax.ShapeDtypeStruct((m, G_VAL_WIDTH), BF16),
                   jax.ShapeDtypeStruct((m // G_CHUNK, 4, G_KEY_WIDTH), F32)],
        compiler_params=pltpu.CompilerParams(dimension_semantics=("arbitrary",),
                                             vmem_limit_bytes=56 * MIB),
        name="l1_proj",
    )(x2d, lr, w_bf16, wg, bg, tril, triu)


def _gla_direction(p_ref, v_ref, ed_ref, state_ref, backward):
    rows = lax.broadcasted_iota(jnp.int32, (G_CHUNK, G_CHUNK), 0)
    cols = lax.broadcasted_iota(jnp.int32, (G_CHUNK, G_CHUNK), 1)
    eye = (rows == cols).astype(F32)
    mask = rows < cols if backward else rows >= cols
    d = 1 if backward else 0
    order = list(range(SCAN_CHUNKS))[::-1] if backward else list(range(SCAN_CHUNKS))
    kw = G_KEY_WIDTH
    heads = range(G_HEADS)
    rsl = lambda c: slice(c * G_CHUNK, (c + 1) * G_CHUNK)
    ksl = lambda part, h: slice(part * kw + h * G_KEY_DIM, part * kw + (h + 1) * G_KEY_DIM)
    vsl = lambda h: slice(h * G_VAL_DIM, (h + 1) * G_VAL_DIM)
    atts = {(c, h): jnp.where(mask, _dot_nt(p_ref[rsl(c), ksl(0, h)], p_ref[rsl(c), ksl(1, h)]), 0.0).astype(BF16)
            for c in order for h in heads}
    states = [state_ref[h] for h in heads]
    outs = {}
    for c in order:
        inters = []
        for h in heads:
            e_row = ed_ref[c, 2 * d:2 * d + 1, h * G_KEY_DIM:(h + 1) * G_KEY_DIM]
            d_row = ed_ref[c, 2 * d + 1:2 * d + 2, h * G_KEY_DIM:(h + 1) * G_KEY_DIM]
            e_col = jnp.sum(eye * e_row, axis=1, keepdims=True)
            d_col = jnp.sum(eye * d_row, axis=1, keepdims=True)
            inters.append(_dot(p_ref[rsl(c), ksl(0, h)], (states[h] * e_col).astype(BF16)))
            states[h] = states[h] * d_col + _dot_tn(p_ref[rsl(c), ksl(2, h)], v_ref[rsl(c), vsl(h)])
        for h in heads:
            outs[c, h] = _dot(atts[c, h], v_ref[rsl(c), vsl(h)]) + inters[h]
    for h in heads:
        state_ref[h] = states[h]
    return outs


def _gla_bwd_kernel(p_ref, v_ref, ed_ref, o_ref, state_ref):
    @pl.when(pl.program_id(1) == 0)
    def _():
        state_ref[...] = jnp.zeros_like(state_ref)

    outs = _gla_direction(p_ref, v_ref, ed_ref, state_ref, backward=True)
    for (c, h), o in outs.items():
        o_ref[c * G_CHUNK:(c + 1) * G_CHUNK, h * G_VAL_DIM:(h + 1) * G_VAL_DIM] = o.astype(BF16)


def _gla_bwd(p, v, ed, n_seq, n_step):
    m = v.shape[0]
    rev = lambda b, j: (b * n_step + n_step - 1 - j, 0)
    rev_p = lambda b, j: (b * n_step + n_step - 1 - j, 1)
    rev3 = lambda b, j: (b * n_step + n_step - 1 - j, 0, 0)
    return pl.pallas_call(
        _gla_bwd_kernel,
        grid=(n_seq, n_step),
        in_specs=[pl.BlockSpec((SCAN_ROWS, 3 * G_KEY_WIDTH), rev_p),
                  pl.BlockSpec((SCAN_ROWS, G_VAL_WIDTH), rev),
                  pl.BlockSpec((SCAN_CHUNKS, 4, G_KEY_WIDTH), rev3)],
        out_specs=pl.BlockSpec((SCAN_ROWS, G_VAL_WIDTH), rev),
        out_shape=jax.ShapeDtypeStruct((m, G_VAL_WIDTH), BF16),
        scratch_shapes=[pltpu.VMEM((G_HEADS, G_KEY_DIM, G_VAL_DIM), F32)],
        compiler_params=pltpu.CompilerParams(dimension_semantics=("arbitrary", "arbitrary"),
                                             vmem_limit_bytes=32 * MIB),
        name="gla_bwd",
    )(p, v, ed)


def _gla_fwd_kernel(p_ref, v_ref, ed_ref, ob_ref, g_ref, x_ref, hn_ref, wo_ref, lng_ref, lnb_ref,
                    o_ref, state_ref):
    @pl.when(pl.program_id(1) == 0)
    def _():
        state_ref[...] = jnp.zeros_like(state_ref)

    outs = _gla_direction(p_ref, v_ref, ed_ref, state_ref, backward=False)
    acts = []
    for c in range(SCAN_CHUNKS):
        rsl = slice(c * G_CHUNK, (c + 1) * G_CHUNK)
        segs = []
        for h in range(G_HEADS):
            seg = outs[c, h] + ob_ref[rsl, h * G_VAL_DIM:(h + 1) * G_VAL_DIM].astype(F32)
            ms = jnp.mean(seg * seg, axis=-1, keepdims=True)
            segs.append(seg * lax.rsqrt(ms + RMS_EPS))
        on = jnp.concatenate(segs, axis=1) * hn_ref[...]
        acts.append((on * _silu(g_ref[rsl, :].astype(F32))).astype(BF16))
    y = DN_ALPHA * x_ref[...] + _dot(jnp.concatenate(acts, axis=0), wo_ref[...])
    o_ref[...] = _layer_norm(y, lng_ref[...], lnb_ref[...])


def _gla_fwd(p, v, ed, o_b, gate, x2d, head_norm, wo_bf16, ln_g, ln_b, n_seq, n_step):
    m = x2d.shape[0]
    row = lambda b, j: (b * n_step + j, 0)
    row3 = lambda b, j: (b * n_step + j, 0, 0)
    const2 = lambda b, j: (0, 0)
    return pl.pallas_call(
        _gla_fwd_kernel,
        grid=(n_seq, n_step),
        in_specs=[pl.BlockSpec((SCAN_ROWS, 3 * G_KEY_WIDTH), row),
                  pl.BlockSpec((SCAN_ROWS, G_VAL_WIDTH), row),
                  pl.BlockSpec((SCAN_CHUNKS, 4, G_KEY_WIDTH), row3),
                  pl.BlockSpec((SCAN_ROWS, G_VAL_WIDTH), row),
                  pl.BlockSpec((SCAN_ROWS, G_VAL_WIDTH), row),
                  pl.BlockSpec((SCAN_ROWS, D_MODEL), row),
                  pl.BlockSpec((1, G_VAL_WIDTH), const2),
                  pl.BlockSpec((G_VAL_WIDTH, D_MODEL), const2),
                  pl.BlockSpec((1, D_MODEL), const2),
                  pl.BlockSpec((1, D_MODEL), const2)],
        out_specs=pl.BlockSpec((SCAN_ROWS, D_MODEL), row),
        out_shape=jax.ShapeDtypeStruct((m, D_MODEL), F32),
        scratch_shapes=[pltpu.VMEM((G_HEADS, G_KEY_DIM, G_VAL_DIM), F32)],
        compiler_params=pltpu.CompilerParams(dimension_semantics=("arbitrary", "arbitrary"),
                                             vmem_limit_bytes=40 * MIB),
        name="gla_fwd",
    )(p, v, ed, o_b, gate, x2d, head_norm, wo_bf16, ln_g, ln_b)


def _trunk(x, l0, l1):
    n_seq, seq, _ = x.shape
    assert seq % PROJ_ROWS == 0 and seq % L0_ROWS == 0 and seq % SCAN_ROWS == 0
    x2d = x.reshape(n_seq * seq, D_MODEL)
    x1, lr = _l0_layer(x2d, l0["sink"], l0["w_in"], l0["bias"], l0["w_out"], l0["ln_g"], l0["ln_b"],
                       l1["w_lr"], seq // L0_ROWS)
    p, v, gate1, ed = _l1_proj(x1, lr, l1["w_in"], l1["w_gate"], l1["b_gate"], l1["tril"], l1["triu"])
    o_b = _gla_bwd(p, v, ed, n_seq, seq // SCAN_ROWS)
    y = _gla_fwd(p, v, ed, o_b, gate1, x1, l1["head_norm"], l1["w_out"], l1["ln_g"], l1["ln_b"],
                 n_seq, seq // SCAN_ROWS)
    return y.reshape(n_seq, seq, D_MODEL)


def kernel(x_prompt, x_sample, l0_w_in, l0_sink, l0_w_out, l0_ln_g, l0_ln_b,
           l1_w_in, l1_w_gate_f, l1_b_gate_f, l1_w_gate_b, l1_b_gate_b, l1_head_norm,
           l1_w_out, l1_ln_g, l1_ln_b):
    assert x_prompt.shape[1] == x_sample.shape[1]
    row = lambda t: t.reshape(1, -1).astype(F32)
    l0 = dict(w_in=l0_w_in.astype(BF16), sink=l0_sink.astype(F32), bias=_attn_bias_table(),
              w_out=l0_w_out.astype(BF16), ln_g=row(l0_ln_g), ln_b=row(l0_ln_b))
    n_main = 2 * G_KEY_WIDTH + 2 * G_VAL_WIDTH
    w_lr = jnp.zeros((D_MODEL, 128), BF16).at[:, :2 * G_RANK].set(l1_w_in[:, n_main:].astype(BF16))
    w_gate = jnp.zeros((128, 2 * G_KEY_WIDTH), BF16)
    w_gate = w_gate.at[:G_RANK, :G_KEY_WIDTH].set(l1_w_gate_f.astype(BF16))
    w_gate = w_gate.at[G_RANK:2 * G_RANK, G_KEY_WIDTH:].set(l1_w_gate_b.astype(BF16))
    b_gate = jnp.concatenate([l1_b_gate_f, l1_b_gate_b]).reshape(1, -1).astype(F32)
    tri = np.tril(np.ones((G_CHUNK, G_CHUNK), np.float32))
    l1 = dict(w_in=l1_w_in[:, :n_main].astype(BF16), w_lr=w_lr, w_gate=w_gate, b_gate=b_gate,
              tril=jnp.asarray(np.concatenate([tri, tri], axis=1), BF16),
              triu=jnp.asarray(np.concatenate([tri.T, tri.T], axis=1), BF16),
              head_norm=row(l1_head_norm), w_out=l1_w_out.astype(BF16),
              ln_g=row(l1_ln_g), ln_b=row(l1_ln_b))
    return (_trunk(x_prompt, l0, l1), _trunk(x_sample, l0, l1))
```

```python
import functools

import jax
import jax.numpy as jnp
import numpy as np
from jax import lax
from jax.experimental import pallas as pl
from jax.experimental.pallas import tpu as pltpu

F32 = jnp.float32
BF16 = jnp.bfloat16

D_MODEL = 1024
DEPTH = 2
LN_EPS = 1e-5
RMS_EPS = 1e-6
DN_ALPHA = (2 * DEPTH) ** 0.25
NEG = -1e30
LOG2E = 1.4426950408889634

A_HEADS = 16
A_KV_HEADS = 4
A_HEAD_DIM = 64
A_REP = A_HEADS // A_KV_HEADS
A_KV_WIDTH = A_KV_HEADS * A_HEAD_DIM
A_BLOCK = 128
A_PAIRS = A_HEADS // 2
A_WIN = 3 * A_BLOCK
A_QK_AHEAD = 3

G_HEADS = 4
G_KEY_DIM = 128
G_VAL_DIM = 256
G_KEY_WIDTH = G_HEADS * G_KEY_DIM
G_VAL_WIDTH = G_HEADS * G_VAL_DIM
G_RANK = 16
G_TAU = 16.0
G_CHUNK = 128
G_MID = G_CHUNK // 2

L0_SUB = 2
L0_ROWS = L0_SUB * A_BLOCK
SCAN_CHUNKS = 4
SCAN_ROWS = SCAN_CHUNKS * G_CHUNK
MIB = 1024 * 1024


def _layer_norm(y, g, b):
    mu = jnp.mean(y, axis=-1, keepdims=True)
    yc = y - mu
    var = jnp.mean(yc * yc, axis=-1, keepdims=True)
    return yc * lax.rsqrt(var + LN_EPS) * g + b


def _silu(g):
    h = 0.5 * g
    return h + h * jnp.tanh(h)


def _dot(a, b):
    return jnp.dot(a, b, preferred_element_type=F32)


def _dot_nt(a, b):
    return lax.dot_general(a, b, (((1,), (1,)), ((), ())), preferred_element_type=F32)


def _dot_tn(a, b):
    return lax.dot_general(a, b, (((0,), (0,)), ((), ())), preferred_element_type=F32)


def _l0_kernel(sink_ref, x_ref, xo_ref, w_ref, bias_a_ref, bias_b_ref, wo_ref, lng_ref, lnb_ref, wlr_ref,
               o_ref, lr_ref, q_s, g_s, kv_s, a_s):
    t = pl.program_id(0)

    @pl.when(t == 0)
    def _():
        q_s[...] = jnp.zeros_like(q_s)
        g_s[...] = jnp.zeros_like(g_s)
        kv_s[...] = jnp.zeros_like(kv_s)
        a_s[...] = jnp.zeros_like(a_s)

    new2, old2 = lax.rem(t, 2), lax.rem(t + 1, 2)
    k_new, k_cur, k_old = lax.rem(t, 3), lax.rem(t + 2, 3), lax.rem(t + 1, 3)
    q_blk = q_s[old2]
    g_blk = g_s[old2]
    kv_cur = kv_s[k_cur]
    kv_old_tail = kv_s[k_old, A_BLOCK:, :]
    a_prev = a_s[...]
    xb = x_ref[...].astype(BF16)

    zeros = jnp.zeros((A_WIN, A_HEAD_DIM), BF16)
    lane = lax.broadcasted_iota(jnp.int32, (A_BLOCK, 2 * A_HEAD_DIM), 1)
    first_half = lane < A_HEAD_DIM
    first_row_half = lax.broadcasted_iota(jnp.int32, (1, 2 * A_HEAD_DIM), 1) < A_HEAD_DIM

    ones_row = lax.broadcasted_iota(jnp.int32, (2 * A_WIN, 2 * A_HEAD_DIM), 0) < A_WIN
    ones_lane = lax.broadcasted_iota(jnp.int32, (2 * A_WIN, 2 * A_HEAD_DIM), 1) < A_HEAD_DIM
    ones_blk = (ones_row == ones_lane).astype(BF16)

    def slabs(prev, cur, nxt):
        win = jnp.concatenate([prev, cur, nxt], axis=0)
        kk, vv = [], []
        for g in range(A_KV_HEADS):
            kg = win[:, g * A_HEAD_DIM:(g + 1) * A_HEAD_DIM]
            vg = win[:, A_KV_WIDTH + g * A_HEAD_DIM:A_KV_WIDTH + (g + 1) * A_HEAD_DIM]
            kk.append(jnp.concatenate([jnp.concatenate([kg, zeros], axis=1),
                                       jnp.concatenate([zeros, kg], axis=1)], axis=0))
            vv.append(jnp.concatenate([jnp.concatenate([jnp.concatenate([vg, zeros], axis=1),
                                                        jnp.concatenate([zeros, vg], axis=1)], axis=0),
                                       ones_blk], axis=1))
        return kk, vv

    bias_refs = (bias_a_ref, bias_b_ref)
    win_slabs = [slabs(kv_old_tail, kv_cur[:A_BLOCK], kv_cur[A_BLOCK:]), None]
    n_stage = L0_SUB * A_PAIRS

    def scores(j):
        sub, i = divmod(j, A_PAIRS)
        qp = q_blk[sub * A_BLOCK:(sub + 1) * A_BLOCK, i * 128:(i + 1) * 128]
        return _dot_nt(qp, win_slabs[sub][0][i // (A_REP // 2)])

    def softmax(j, s):
        sub, i = divmod(j, A_PAIRS)
        t_ = s + bias_refs[sub][0, i]
        ps, ms = [], []
        for hh in range(2):
            th = t_[:, hh * A_WIN:(hh + 1) * A_WIN]
            m = jnp.max(th, axis=-1, keepdims=True)
            ps.append(jnp.exp2(th - m).astype(BF16))
            ms.append(m)
        sink = jnp.where(first_row_half, sink_ref[2 * i] * LOG2E, sink_ref[2 * i + 1] * LOG2E)
        return jnp.concatenate(ps, axis=1), jnp.exp2(sink - jnp.where(first_half, ms[0], ms[1]))

    def weighted_values(j, p, sink_term):
        sub, i = divmod(j, A_PAIRS)
        rows = slice(sub * A_BLOCK, (sub + 1) * A_BLOCK)
        pv = _dot(p, win_slabs[sub][1][i // (A_REP // 2)])
        o = pv[:, :128] * (1.0 / (pv[:, 128:] + sink_term))
        a_s[rows, i * 128:(i + 1) * 128] = (o * _silu(g_blk[rows, i * 128:(i + 1) * 128].astype(F32))).astype(BF16)

    y_chunks = []
    fw = D_MODEL // 4

    def out_proj_chunk(c):
        y_chunks.append(_dot(a_prev, wo_ref[:, c * fw:(c + 1) * fw]))
        if c == 3:
            y = DN_ALPHA * xo_ref[...] + jnp.concatenate(y_chunks, axis=1)
            y_chunks[:] = [_layer_norm(y, lng_ref[...], lnb_ref[...])]
            o_ref[...] = y_chunks[0]

    def low_rank_chunk(c):
        lr_ref[...] = _dot(y_chunks[0].astype(BF16), wlr_ref[...]).astype(BF16)

    def q_chunk(c):
        acc = _dot(xb, w_ref[:, c * fw:(c + 1) * fw])
        q_s[new2, :, c * fw:(c + 1) * fw] = (acc * (A_HEAD_DIM ** -0.5 * LOG2E)).astype(BF16)

    def gate_chunk(c):
        c0 = D_MODEL + 2 * A_KV_WIDTH + c * fw
        g_s[new2, :, c * fw:(c + 1) * fw] = _dot(xb, w_ref[:, c0:c0 + fw]).astype(BF16)

    kv_parts = []

    def kv_chunk(c):
        c0 = D_MODEL + c * A_KV_WIDTH
        kv_parts.append(_dot(xb, w_ref[:, c0:c0 + A_KV_WIDTH]).astype(BF16))
        if c == 1:
            kv_new = jnp.concatenate(kv_parts, axis=1)
            kv_s[k_new] = kv_new
            win_slabs[1] = slabs(kv_cur[:A_BLOCK], kv_cur[A_BLOCK:], kv_new[:A_BLOCK])

    fillers = [(kv_chunk, c) for c in range(2)] + [(out_proj_chunk, c) for c in range(4)]
    assert len(fillers) <= n_stage and A_QK_AHEAD + 2 <= A_PAIRS

    fillers += ([(q_chunk, c) for c in range(3)] + [(low_rank_chunk, 0), (q_chunk, 3)]
                + [(gate_chunk, c) for c in range(4)])
    assert len(fillers) <= n_stage

    s_ready = {j: scores(j) for j in range(A_QK_AHEAD)}
    pending = None
    for j in range(n_stage):
        if j + A_QK_AHEAD < n_stage:
            s_ready[j + A_QK_AHEAD] = scores(j + A_QK_AHEAD)
        p, sink_term = softmax(j, s_ready.pop(j))
        if pending is not None:
            weighted_values(*pending)
        pending = (j, p, sink_term)
        if j < len(fillers):
            fn, c = fillers[j]
            fn(c)
    weighted_values(*pending)


def _l0_layer(x2d, sink, w_bf16, bias, wo_bf16, ln_g, ln_b, w_lr, blocks_per_seq):
    m = x2d.shape[0]
    done = lambda t: (jnp.maximum(t - 2, 0), 0)
    n_blk = m // L0_ROWS
    n_w = w_bf16.shape[1]
    const2 = lambda t: (0, 0)
    var_a = lambda t: (jnp.where(lax.rem(t + blocks_per_seq - 1, blocks_per_seq) == 0, 0, 1), 0, 0, 0)
    var_b = lambda t: (jnp.where(lax.rem(t, blocks_per_seq) == 0, 2, 1), 0, 0, 0)
    return pl.pallas_call(
        _l0_kernel,
        grid=(n_blk + 2,),
        in_specs=[pl.BlockSpec(memory_space=pltpu.SMEM),
                  pl.BlockSpec((L0_ROWS, D_MODEL), lambda t: (jnp.minimum(t, n_blk - 1), 0)),
                  pl.BlockSpec((L0_ROWS, D_MODEL), done),
                  pl.BlockSpec((D_MODEL, n_w), const2),
                  pl.BlockSpec((1, A_PAIRS, A_BLOCK, 2 * A_WIN), var_a),
                  pl.BlockSpec((1, A_PAIRS, A_BLOCK, 2 * A_WIN), var_b),
                  pl.BlockSpec((D_MODEL, D_MODEL), const2),
                  pl.BlockSpec((1, D_MODEL), const2),
                  pl.BlockSpec((1, D_MODEL), const2),
                  pl.BlockSpec((D_MODEL, 128), const2)],
        out_specs=[pl.BlockSpec((L0_ROWS, D_MODEL), done),
                   pl.BlockSpec((L0_ROWS, 128), done)],
        out_shape=[jax.ShapeDtypeStruct((m, D_MODEL), F32),
                   jax.ShapeDtypeStruct((m, 128), BF16)],
        scratch_shapes=[pltpu.VMEM((2, L0_ROWS, D_MODEL), BF16),
                        pltpu.VMEM((2, L0_ROWS, D_MODEL), BF16),
                        pltpu.VMEM((3, L0_ROWS, 2 * A_KV_WIDTH), BF16),
                        pltpu.VMEM((L0_ROWS, D_MODEL), BF16)],
        compiler_params=pltpu.CompilerParams(dimension_semantics=("arbitrary",),
                                             vmem_limit_bytes=56 * MIB),
        name="l0_layer",
    )(sink, x2d, x2d, w_bf16, bias, bias, wo_bf16, ln_g, ln_b, w_lr)


def _attn_bias_table():
    qi = np.arange(A_BLOCK)[:, None] + A_BLOCK
    kj = np.arange(A_WIN)[None, :]
    dist = np.abs(qi - kj).astype(np.float32)
    in_win = dist <= A_BLOCK
    slopes = (2.0 ** (-8.0 * np.arange(1, A_HEADS + 1) / A_HEADS)).astype(np.float32)
    per_head = (-slopes[:, None, None] * dist[None]).astype(np.float32) * np.float32(LOG2E)
    variants = []
    for valid in (kj >= A_BLOCK, kj >= 0, kj < 2 * A_BLOCK):
        b = np.where((in_win & valid)[None], per_head, np.float32(NEG)).astype(np.float32)
        variants.append(b.reshape(A_PAIRS, 2, A_BLOCK, A_WIN).transpose(0, 2, 1, 3).reshape(A_PAIRS, A_BLOCK, 2 * A_WIN))
    return jnp.asarray(np.stack(variants))


def _log2_decay(z):
    return (jnp.minimum(z, 0.0) - jnp.log(1.0 + jnp.exp(-jnp.abs(z)))) * (LOG2E / G_TAU)


def _split_cumsum(tri2_bf16, la):
    hi = la.astype(BF16)
    lo = (la - hi.astype(F32)).astype(BF16)
    return _dot(tri2_bf16, jnp.concatenate([hi, lo], axis=0))


class _GlaScan:
    def __init__(self, p_ref, v_ref, ed_ref, state_ref, backward):
        rows = lax.broadcasted_iota(jnp.int32, (G_CHUNK, G_CHUNK), 0)
        cols = lax.broadcasted_iota(jnp.int32, (G_CHUNK, G_CHUNK), 1)
        self.eye = (rows == cols).astype(F32)
        self.mask = rows < cols if backward else rows >= cols
        self.d = 1 if backward else 0
        self.order = list(range(SCAN_CHUNKS))[::-1] if backward else list(range(SCAN_CHUNKS))
        self.p_ref, self.v_ref, self.ed_ref, self.state_ref = p_ref, v_ref, ed_ref, state_ref
        self.states = [state_ref[h] for h in range(G_HEADS)]

    @staticmethod
    def _rows(c):
        return slice(c * G_CHUNK, (c + 1) * G_CHUNK)

    @staticmethod
    def _kcols(part, h):
        return slice(part * G_KEY_WIDTH + h * G_KEY_DIM, part * G_KEY_WIDTH + (h + 1) * G_KEY_DIM)

    @staticmethod
    def _vcols(h):
        return slice(h * G_VAL_DIM, (h + 1) * G_VAL_DIM)

    def attention(self):
        p = self.p_ref
        self.atts = {(c, h): jnp.where(self.mask,
                                       _dot_nt(p[self._rows(c), self._kcols(0, h)], p[self._rows(c), self._kcols(1, h)]),
                                       0.0).astype(BF16)
                     for c in self.order for h in range(G_HEADS)}

    def chunk(self, c):
        p, v, ed, d = self.p_ref, self.v_ref, self.ed_ref, self.d
        inters = []
        for h in range(G_HEADS):
            e_row = ed[c, 2 * d:2 * d + 1, h * G_KEY_DIM:(h + 1) * G_KEY_DIM]
            d_row = ed[c, 2 * d + 1:2 * d + 2, h * G_KEY_DIM:(h + 1) * G_KEY_DIM]
            e_col = jnp.sum(self.eye * e_row, axis=1, keepdims=True)
            d_col = jnp.sum(self.eye * d_row, axis=1, keepdims=True)
            inters.append(_dot(p[self._rows(c), self._kcols(0, h)], (self.states[h] * e_col).astype(BF16)))
            self.states[h] = (self.states[h] * d_col
                              + _dot_tn(p[self._rows(c), self._kcols(2, h)], v[self._rows(c), self._vcols(h)]))
        return [_dot(self.atts[c, h], v[self._rows(c), self._vcols(h)]) + inters[h] for h in range(G_HEADS)]

    def finish(self):
        for h in range(G_HEADS):
            self.state_ref[h] = self.states[h]


def _l1_bwd_kernel(x_ref, lr_ref, w_ref, wg_ref, bg_ref, tril_ref, triu_ref,
                   pf_ref, v_ref, g_ref, ed_ref, ob_ref,
                   pb_s, v_s, ed_s, state_s, *, n_blk, blocks_per_seq):
    t = pl.program_id(0)

    @pl.when(t == 0)
    def _():
        pb_s[...] = jnp.zeros_like(pb_s)
        v_s[...] = jnp.zeros_like(v_s)
        ed_s[...] = jnp.zeros_like(ed_s)

    @pl.when((t == 0) | (lax.rem(n_blk - t + blocks_per_seq, blocks_per_seq) == blocks_per_seq - 1))
    def _():
        state_s[...] = jnp.zeros_like(state_s)

    kw = G_KEY_WIDTH
    scan = _GlaScan(pb_s, v_s, ed_s, state_s, backward=True)
    xb = x_ref[...].astype(BF16)
    z = _dot(lr_ref[...], wg_ref[...]) + bg_ref[...]
    scan.attention()
    q = _dot(xb, w_ref[:, 0:kw]) * (G_KEY_DIM ** -0.5)
    outs = {scan.order[0]: scan.chunk(scan.order[0])}
    k = _dot(xb, w_ref[:, kw:2 * kw])
    la = _log2_decay(z)
    cums = [(_split_cumsum(tril_ref[...], la[c * G_CHUNK:(c + 1) * G_CHUNK, 0:kw]),
             _split_cumsum(triu_ref[...], la[c * G_CHUNK:(c + 1) * G_CHUNK, kw:2 * kw]))
            for c in range(SCAN_CHUNKS)]
    outs[scan.order[1]] = scan.chunk(scan.order[1])
    v_new = jnp.concatenate([_dot(xb, w_ref[:, 2 * kw + c * kw:2 * kw + (c + 1) * kw]).astype(BF16)
                             for c in range(2)], axis=1)
    v_ref[...] = v_new
    for c in scan.order[2:-1]:
        outs[c] = scan.chunk(c)
    pb_new, ed_new = [], []
    for c in range(SCAN_CHUNKS):
        r0 = c * G_CHUNK
        qc = q[r0:r0 + G_CHUNK]
        kc = k[r0:r0 + G_CHUNK]
        b, s = cums[c]
        b_mid = b[G_MID - 1:G_MID]
        b_end = b[G_CHUNK - 1:G_CHUNK]
        pf_ref[r0:r0 + G_CHUNK, 0:kw] = (qc * jnp.exp2(b - b_mid)).astype(BF16)
        pf_ref[r0:r0 + G_CHUNK, kw:2 * kw] = (kc * jnp.exp2(b_mid - b)).astype(BF16)
        pf_ref[r0:r0 + G_CHUNK, 2 * kw:3 * kw] = (kc * jnp.exp2(b_end - b)).astype(BF16)
        s_mid = s[G_MID:G_MID + 1]
        s_end = s[0:1]
        pb_new.append(jnp.concatenate([(qc * jnp.exp2(s - s_mid)).astype(BF16),
                                       (kc * jnp.exp2(s_mid - s)).astype(BF16),
                                       (kc * jnp.exp2(s_end - s)).astype(BF16)], axis=1))
        ed_new.append(jnp.concatenate([jnp.exp2(b_mid), jnp.exp2(b_end), jnp.exp2(s_mid), jnp.exp2(s_end)], axis=0))
        ed_ref[c] = ed_new[c]
    for c in range(2):
        c0 = 2 * kw + G_VAL_WIDTH + c * kw
        g_ref[:, c * kw:(c + 1) * kw] = _dot(xb, w_ref[:, c0:c0 + kw]).astype(BF16)
    outs[scan.order[-1]] = scan.chunk(scan.order[-1])
    scan.finish()
    for c, per_head in outs.items():
        for h, o in enumerate(per_head):
            ob_ref[c * G_CHUNK:(c + 1) * G_CHUNK, h * G_VAL_DIM:(h + 1) * G_VAL_DIM] = o.astype(BF16)
    for c in range(SCAN_CHUNKS):
        pb_s[c * G_CHUNK:(c + 1) * G_CHUNK, :] = pb_new[c]
        ed_s[c] = ed_new[c]
    v_s[...] = v_new


def _l1_bwd(x2d, lr, w_bf16, wg, bg, tril, triu, blocks_per_seq):
    m = x2d.shape[0]
    n_blk = m // SCAN_ROWS
    n_main = w_bf16.shape[1]
    const2 = lambda t: (0, 0)
    proj = lambda t: (jnp.maximum(n_blk - 1 - t, 0), 0)
    proj3 = lambda t: (jnp.maximum(n_blk - 1 - t, 0), 0, 0)
    scanned = lambda t: (jnp.minimum(n_blk - t, n_blk - 1), 0)
    return pl.pallas_call(
        functools.partial(_l1_bwd_kernel, n_blk=n_blk, blocks_per_seq=blocks_per_seq),
        grid=(n_blk + 1,),
        in_specs=[pl.BlockSpec((SCAN_ROWS, D_MODEL), proj),
                  pl.BlockSpec((SCAN_ROWS, 128), proj),
                  pl.BlockSpec((D_MODEL, n_main), const2),
                  pl.BlockSpec((128, 2 * G_KEY_WIDTH), const2),
                  pl.BlockSpec((1, 2 * G_KEY_WIDTH), const2),
                  pl.BlockSpec((G_CHUNK, 2 * G_CHUNK), const2),
                  pl.BlockSpec((G_CHUNK, 2 * G_CHUNK), const2)],
        out_specs=[pl.BlockSpec((SCAN_ROWS, 3 * G_KEY_WIDTH), proj),
                   pl.BlockSpec((SCAN_ROWS, G_VAL_WIDTH), proj),
                   pl.BlockSpec((SCAN_ROWS, G_VAL_WIDTH), proj),
                   pl.BlockSpec((SCAN_CHUNKS, 4, G_KEY_WIDTH), proj3),
                   pl.BlockSpec((SCAN_ROWS, G_VAL_WIDTH), scanned)],
        out_shape=[jax.ShapeDtypeStruct((m, 3 * G_KEY_WIDTH), BF16),
                   jax.ShapeDtypeStruct((m, G_VAL_WIDTH), BF16),
                   jax.ShapeDtypeStruct((m, G_VAL_WIDTH), BF16),
                   jax.ShapeDtypeStruct((m // G_CHUNK, 4, G_KEY_WIDTH), F32),
                   jax.ShapeDtypeStruct((m, G_VAL_WIDTH), BF16)],
        scratch_shapes=[pltpu.VMEM((SCAN_ROWS, 3 * G_KEY_WIDTH), BF16),
                        pltpu.VMEM((SCAN_ROWS, G_VAL_WIDTH), BF16),
                        pltpu.VMEM((SCAN_CHUNKS, 4, G_KEY_WIDTH), F32),
                        pltpu.VMEM((G_HEADS, G_KEY_DIM, G_VAL_DIM), F32)],
        compiler_params=pltpu.CompilerParams(dimension_semantics=("arbitrary",),
                                             vmem_limit_bytes=56 * MIB),
        name="l1_bwd",
    )(x2d, lr, w_bf16, wg, bg, tril, triu)


def _gla_fwd_kernel(p_ref, v_ref, ed_ref, ob_ref, g_ref, x_ref, hn_ref, wo_ref, lng_ref, lnb_ref,
                    o_ref, state_ref):
    @pl.when(pl.program_id(1) == 0)
    def _():
        state_ref[...] = jnp.zeros_like(state_ref)

    scan = _GlaScan(p_ref, v_ref, ed_ref, state_ref, backward=False)
    scan.attention()
    outs = {c: scan.chunk(c) for c in scan.order}
    scan.finish()
    acts = []
    for c in range(SCAN_CHUNKS):
        rsl = slice(c * G_CHUNK, (c + 1) * G_CHUNK)
        segs = []
        for h in range(G_HEADS):
            seg = outs[c][h] + ob_ref[rsl, h * G_VAL_DIM:(h + 1) * G_VAL_DIM].astype(F32)
            ms = jnp.mean(seg * seg, axis=-1, keepdims=True)
            segs.append(seg * lax.rsqrt(ms + RMS_EPS))
        on = jnp.concatenate(segs, axis=1) * hn_ref[...]
        acts.append((on * _silu(g_ref[rsl, :].astype(F32))).astype(BF16))
    y = DN_ALPHA * x_ref[...] + _dot(jnp.concatenate(acts, axis=0), wo_ref[...])
    o_ref[...] = _layer_norm(y, lng_ref[...], lnb_ref[...])


def _gla_fwd(p, v, ed, o_b, gate, x2d, head_norm, wo_bf16, ln_g, ln_b, n_seq, n_step):
    m = x2d.shape[0]
    row = lambda b, j: (b * n_step + j, 0)
    row3 = lambda b, j: (b * n_step + j, 0, 0)
    const2 = lambda b, j: (0, 0)
    return pl.pallas_call(
        _gla_fwd_kernel,
        grid=(n_seq, n_step),
        in_specs=[pl.BlockSpec((SCAN_ROWS, 3 * G_KEY_WIDTH), row),
                  pl.BlockSpec((SCAN_ROWS, G_VAL_WIDTH), row),
                  pl.BlockSpec((SCAN_CHUNKS, 4, G_KEY_WIDTH), row3),
                  pl.BlockSpec((SCAN_ROWS, G_VAL_WIDTH), row),
                  pl.BlockSpec((SCAN_ROWS, G_VAL_WIDTH), row),
                  pl.BlockSpec((SCAN_ROWS, D_MODEL), row),
                  pl.BlockSpec((1, G_VAL_WIDTH), const2),
                  pl.BlockSpec((G_VAL_WIDTH, D_MODEL), const2),
                  pl.BlockSpec((1, D_MODEL), const2),
                  pl.BlockSpec((1, D_MODEL), const2)],
        out_specs=pl.BlockSpec((SCAN_ROWS, D_MODEL), row),
        out_shape=jax.ShapeDtypeStruct((m, D_MODEL), F32),
        scratch_shapes=[pltpu.VMEM((G_HEADS, G_KEY_DIM, G_VAL_DIM), F32)],
        compiler_params=pltpu.CompilerParams(dimension_semantics=("arbitrary", "arbitrary"),
                                             vmem_limit_bytes=56 * MIB),
        name="gla_fwd",
    )(p, v, ed, o_b, gate, x2d, head_norm, wo_bf16, ln_g, ln_b)


def _trunk(x, l0, l1):
    n_seq, seq, _ = x.shape
    assert seq % L0_ROWS == 0 and seq % SCAN_ROWS == 0 and SCAN_CHUNKS == 4
    x2d = x.reshape(n_seq * seq, D_MODEL)
    x1, lr = _l0_layer(x2d, l0["sink"], l0["w_in"], l0["bias"], l0["w_out"], l0["ln_g"], l0["ln_b"],
                       l1["w_lr"], seq // L0_ROWS)
    p_f, v, gate1, ed, o_b = _l1_bwd(x1, lr, l1["w_in"], l1["w_gate"], l1["b_gate"], l1["tril"], l1["triu"],
                                     seq // SCAN_ROWS)
    y = _gla_fwd(p_f, v, ed, o_b, gate1, x1, l1["head_norm"], l1["w_out"], l1["ln_g"], l1["ln_b"],
                 n_seq, seq // SCAN_ROWS)
    return y.reshape(n_seq, seq, D_MODEL)


def kernel(x_prompt, x_sample, l0_w_in, l0_sink, l0_w_out, l0_ln_g, l0_ln_b,
           l1_w_in, l1_w_gate_f, l1_b_gate_f, l1_w_gate_b, l1_b_gate_b, l1_head_norm,
           l1_w_out, l1_ln_g, l1_ln_b):
    assert x_prompt.shape[1] == x_sample.shape[1]
    row = lambda t: t.reshape(1, -1).astype(F32)
    l0 = dict(w_in=l0_w_in.astype(BF16), sink=l0_sink.astype(F32), bias=_attn_bias_table(),
              w_out=l0_w_out.astype(BF16), ln_g=row(l0_ln_g), ln_b=row(l0_ln_b))
    n_main = 2 * G_KEY_WIDTH + 2 * G_VAL_WIDTH
    w_lr = jnp.zeros((D_MODEL, 128), BF16).at[:, :2 * G_RANK].set(l1_w_in[:, n_main:].astype(BF16))
    w_gate = jnp.zeros((128, 2 * G_KEY_WIDTH), BF16)
    w_gate = w_gate.at[:G_RANK, :G_KEY_WIDTH].set(l1_w_gate_f.astype(BF16))
    w_gate = w_gate.at[G_RANK:2 * G_RANK, G_KEY_WIDTH:].set(l1_w_gate_b.astype(BF16))
    b_gate = jnp.concatenate([l1_b_gate_f, l1_b_gate_b]).reshape(1, -1).astype(F32)
    tri = np.tril(np.ones((G_CHUNK, G_CHUNK), np.float32))
    l1 = dict(w_in=l1_w_in[:, :n_main].astype(BF16), w_lr=w_lr, w_gate=w_gate, b_gate=b_gate,
              tril=jnp.asarray(np.concatenate([tri, tri], axis=1), BF16),
              triu=jnp.asarray(np.concatenate([tri.T, tri.T], axis=1), BF16),
              head_norm=row(l1_head_norm), w_out=l1_w_out.astype(BF16),
              ln_g=row(l1_ln_g), ln_b=row(l1_ln_b))
    return (_trunk(x_prompt, l0, l1), _trunk(x_sample, l0, l1))
```

```python
import functools

import jax
import jax.numpy as jnp
import numpy as np
from jax import lax
from jax.experimental import pallas as pl
from jax.experimental.pallas import tpu as pltpu

F32 = jnp.float32
BF16 = jnp.bfloat16

D_MODEL = 1024
DEPTH = 2
LN_EPS = 1e-5
RMS_EPS = 1e-6
DN_ALPHA = (2 * DEPTH) ** 0.25
NEG = -1e30
LOG2E = 1.4426950408889634

A_HEADS = 16
A_KV_HEADS = 4
A_HEAD_DIM = 64
A_REP = A_HEADS // A_KV_HEADS
A_KV_WIDTH = A_KV_HEADS * A_HEAD_DIM
A_BLOCK = 128
A_PAIRS = A_HEADS // 2
A_WIN = 3 * A_BLOCK
A_QK_AHEAD = 3

G_HEADS = 4
G_KEY_DIM = 128
G_VAL_DIM = 256
G_KEY_WIDTH = G_HEADS * G_KEY_DIM
G_VAL_WIDTH = G_HEADS * G_VAL_DIM
G_RANK = 16
G_TAU = 16.0
G_CHUNK = 128
G_MID = G_CHUNK // 2

L0_SUB = 2
L0_ROWS = L0_SUB * A_BLOCK
SCAN_CHUNKS = 4
SCAN_ROWS = SCAN_CHUNKS * G_CHUNK
V7X_VMEM_BYTES = 64 * 1024 * 1024
VMEM_LIMIT_BYTES = V7X_VMEM_BYTES * 7 // 8


def _layer_norm(y, g, b):
    mu = jnp.mean(y, axis=-1, keepdims=True)
    yc = y - mu
    var = jnp.mean(yc * yc, axis=-1, keepdims=True)
    return yc * lax.rsqrt(var + LN_EPS) * g + b


def _silu(g):
    h = 0.5 * g
    return h + h * jnp.tanh(h)


def _dot(a, b):
    return jnp.dot(a, b, preferred_element_type=F32)


def _dot_nt(a, b):
    return lax.dot_general(a, b, (((1,), (1,)), ((), ())), preferred_element_type=F32)


def _dot_tn(a, b):
    return lax.dot_general(a, b, (((0,), (0,)), ((), ())), preferred_element_type=F32)


def _l0_kernel(sink_ref, x_ref, xo_ref, w_ref, bias_a_ref, bias_b_ref, wo_ref, lng_ref, lnb_ref, wlr_ref,
               o_ref, lr_ref, q_s, g_s, kv_s, a_s):
    t = pl.program_id(0)

    @pl.when(t == 0)
    def _():
        q_s[...] = jnp.zeros_like(q_s)
        g_s[...] = jnp.zeros_like(g_s)
        kv_s[...] = jnp.zeros_like(kv_s)
        a_s[...] = jnp.zeros_like(a_s)

    new2, old2 = lax.rem(t, 2), lax.rem(t + 1, 2)
    k_new, k_cur, k_old = lax.rem(t, 3), lax.rem(t + 2, 3), lax.rem(t + 1, 3)
    q_blk = q_s[old2]
    g_blk = g_s[old2]
    kv_cur = kv_s[k_cur]
    kv_old_tail = kv_s[k_old, A_BLOCK:, :]
    a_prev = a_s[...]
    xb = x_ref[...].astype(BF16)

    zeros = jnp.zeros((A_WIN, A_HEAD_DIM), BF16)
    lane = lax.broadcasted_iota(jnp.int32, (A_BLOCK, 2 * A_HEAD_DIM), 1)
    first_half = lane < A_HEAD_DIM
    first_row_half = lax.broadcasted_iota(jnp.int32, (1, 2 * A_HEAD_DIM), 1) < A_HEAD_DIM

    ones_row = lax.broadcasted_iota(jnp.int32, (2 * A_WIN, 2 * A_HEAD_DIM), 0) < A_WIN
    ones_lane = lax.broadcasted_iota(jnp.int32, (2 * A_WIN, 2 * A_HEAD_DIM), 1) < A_HEAD_DIM
    ones_blk = (ones_row == ones_lane).astype(BF16)

    def slabs(prev, cur, nxt):
        win = jnp.concatenate([prev, cur, nxt], axis=0)
        kk, vv = [], []
        for g in range(A_KV_HEADS):
            kg = win[:, g * A_HEAD_DIM:(g + 1) * A_HEAD_DIM]
            vg = win[:, A_KV_WIDTH + g * A_HEAD_DIM:A_KV_WIDTH + (g + 1) * A_HEAD_DIM]
            kk.append(jnp.concatenate([jnp.concatenate([kg, zeros], axis=1),
                                       jnp.concatenate([zeros, kg], axis=1)], axis=0))
            vv.append(jnp.concatenate([jnp.concatenate([jnp.concatenate([vg, zeros], axis=1),
                                                        jnp.concatenate([zeros, vg], axis=1)], axis=0),
                                       ones_blk], axis=1))
        return kk, vv

    bias_refs = (bias_a_ref, bias_b_ref)
    win_slabs = [slabs(kv_old_tail, kv_cur[:A_BLOCK], kv_cur[A_BLOCK:]), None]
    n_stage = L0_SUB * A_PAIRS

    def scores(j):
        sub, i = divmod(j, A_PAIRS)
        qp = q_blk[sub * A_BLOCK:(sub + 1) * A_BLOCK, i * 128:(i + 1) * 128]
        return _dot_nt(qp, win_slabs[sub][0][i // (A_REP // 2)])

    def softmax(j, s):
        sub, i = divmod(j, A_PAIRS)
        t_ = s + bias_refs[sub][0, i]
        ps, ms = [], []
        for hh in range(2):
            th = t_[:, hh * A_WIN:(hh + 1) * A_WIN]
            m = jnp.max(th, axis=-1, keepdims=True)
            ps.append(jnp.exp2(th - m).astype(BF16))
            ms.append(m)
        sink = jnp.where(first_row_half, sink_ref[2 * i] * LOG2E, sink_ref[2 * i + 1] * LOG2E)
        return jnp.concatenate(ps, axis=1), jnp.exp2(sink - jnp.where(first_half, ms[0], ms[1]))

    def weighted_values(j, p, sink_term):
        sub, i = divmod(j, A_PAIRS)
        rows = slice(sub * A_BLOCK, (sub + 1) * A_BLOCK)
        pv = _dot(p, win_slabs[sub][1][i // (A_REP // 2)])
        o = pv[:, :128] * (1.0 / (pv[:, 128:] + sink_term))
        a_s[rows, i * 128:(i + 1) * 128] = (o * _silu(g_blk[rows, i * 128:(i + 1) * 128].astype(F32))).astype(BF16)

    y_chunks = []
    fw = D_MODEL // 4

    def out_proj_chunk(c):
        y_chunks.append(_dot(a_prev, wo_ref[:, c * fw:(c + 1) * fw]))
        if c == 3:
            y = DN_ALPHA * xo_ref[...] + jnp.concatenate(y_chunks, axis=1)
            y_chunks[:] = [_layer_norm(y, lng_ref[...], lnb_ref[...])]
            o_ref[...] = y_chunks[0]

    def low_rank_chunk(c):
        lr_ref[...] = _dot(y_chunks[0].astype(BF16), wlr_ref[...]).astype(BF16)

    def q_chunk(c):
        acc = _dot(xb, w_ref[:, c * fw:(c + 1) * fw])
        q_s[new2, :, c * fw:(c + 1) * fw] = (acc * (A_HEAD_DIM ** -0.5 * LOG2E)).astype(BF16)

    def gate_chunk(c):
        c0 = D_MODEL + 2 * A_KV_WIDTH + c * fw
        g_s[new2, :, c * fw:(c + 1) * fw] = _dot(xb, w_ref[:, c0:c0 + fw]).astype(BF16)

    kv_parts = []

    def kv_chunk(c):
        c0 = D_MODEL + c * A_KV_WIDTH
        kv_parts.append(_dot(xb, w_ref[:, c0:c0 + A_KV_WIDTH]).astype(BF16))
        if c == 1:
            kv_new = jnp.concatenate(kv_parts, axis=1)
            kv_s[k_new] = kv_new
            win_slabs[1] = slabs(kv_cur[:A_BLOCK], kv_cur[A_BLOCK:], kv_new[:A_BLOCK])

    fillers = [(kv_chunk, c) for c in range(2)] + [(out_proj_chunk, c) for c in range(4)]
    assert len(fillers) <= n_stage and A_QK_AHEAD + 2 <= A_PAIRS

    fillers += ([(q_chunk, c) for c in range(3)] + [(low_rank_chunk, 0), (q_chunk, 3)]
                + [(gate_chunk, c) for c in range(4)])
    assert len(fillers) <= n_stage

    s_ready = {j: scores(j) for j in range(A_QK_AHEAD)}
    pending = None
    for j in range(n_stage):
        if j + A_QK_AHEAD < n_stage:
            s_ready[j + A_QK_AHEAD] = scores(j + A_QK_AHEAD)
        p, sink_term = softmax(j, s_ready.pop(j))
        if pending is not None:
            weighted_values(*pending)
        pending = (j, p, sink_term)
        if j < len(fillers):
            fn, c = fillers[j]
            fn(c)
    weighted_values(*pending)


def _l0_layer(x2d, sink, w_bf16, bias, wo_bf16, ln_g, ln_b, w_lr, blocks_per_seq):
    m = x2d.shape[0]
    done = lambda t: (jnp.maximum(t - 2, 0), 0)
    n_blk = m // L0_ROWS
    n_w = w_bf16.shape[1]
    const2 = lambda t: (0, 0)
    var_a = lambda t: (jnp.where(lax.rem(t + blocks_per_seq - 1, blocks_per_seq) == 0, 0, 1), 0, 0, 0)
    var_b = lambda t: (jnp.where(lax.rem(t, blocks_per_seq) == 0, 2, 1), 0, 0, 0)
    return pl.pallas_call(
        _l0_kernel,
        grid=(n_blk + 2,),
        in_specs=[pl.BlockSpec(memory_space=pltpu.SMEM),
                  pl.BlockSpec((L0_ROWS, D_MODEL), lambda t: (jnp.minimum(t, n_blk - 1), 0)),
                  pl.BlockSpec((L0_ROWS, D_MODEL), done),
                  pl.BlockSpec((D_MODEL, n_w), const2),
                  pl.BlockSpec((1, A_PAIRS, A_BLOCK, 2 * A_WIN), var_a),
                  pl.BlockSpec((1, A_PAIRS, A_BLOCK, 2 * A_WIN), var_b),
                  pl.BlockSpec((D_MODEL, D_MODEL), const2),
                  pl.BlockSpec((1, D_MODEL), const2),
                  pl.BlockSpec((1, D_MODEL), const2),
                  pl.BlockSpec((D_MODEL, 128), const2)],
        out_specs=[pl.BlockSpec((L0_ROWS, D_MODEL), done),
                   pl.BlockSpec((L0_ROWS, 128), done)],
        out_shape=[jax.ShapeDtypeStruct((m, D_MODEL), F32),
                   jax.ShapeDtypeStruct((m, 128), BF16)],
        scratch_shapes=[pltpu.VMEM((2, L0_ROWS, D_MODEL), BF16),
                        pltpu.VMEM((2, L0_ROWS, D_MODEL), BF16),
                        pltpu.VMEM((3, L0_ROWS, 2 * A_KV_WIDTH), BF16),
                        pltpu.VMEM((L0_ROWS, D_MODEL), BF16)],
        compiler_params=pltpu.CompilerParams(dimension_semantics=("arbitrary",),
                                             vmem_limit_bytes=VMEM_LIMIT_BYTES),
        name="l0_layer",
    )(sink, x2d, x2d, w_bf16, bias, bias, wo_bf16, ln_g, ln_b, w_lr)


def _attn_bias_table():
    qi = np.arange(A_BLOCK)[:, None] + A_BLOCK
    kj = np.arange(A_WIN)[None, :]
    dist = np.abs(qi - kj).astype(np.float32)
    in_win = dist <= A_BLOCK
    slopes = (2.0 ** (-8.0 * np.arange(1, A_HEADS + 1) / A_HEADS)).astype(np.float32)
    per_head = (-slopes[:, None, None] * dist[None]).astype(np.float32) * np.float32(LOG2E)
    variants = []
    for valid in (kj >= A_BLOCK, kj >= 0, kj < 2 * A_BLOCK):
        b = np.where((in_win & valid)[None], per_head, np.float32(NEG)).astype(np.float32)
        variants.append(b.reshape(A_PAIRS, 2, A_BLOCK, A_WIN).transpose(0, 2, 1, 3).reshape(A_PAIRS, A_BLOCK, 2 * A_WIN))
    return jnp.asarray(np.stack(variants))


def _log2_decay(z):
    return (jnp.minimum(z, 0.0) - jnp.log(1.0 + jnp.exp(-jnp.abs(z)))) * (LOG2E / G_TAU)


def _split_cumsum(tri2_bf16, la):
    hi = la.astype(BF16)
    lo = (la - hi.astype(F32)).astype(BF16)
    return _dot(tri2_bf16, jnp.concatenate([hi, lo], axis=0))


class _GlaScan:
    def __init__(self, p_ref, v_ref, ed_ref, state_ref, backward):
        rows = lax.broadcasted_iota(jnp.int32, (G_CHUNK, G_CHUNK), 0)
        cols = lax.broadcasted_iota(jnp.int32, (G_CHUNK, G_CHUNK), 1)
        self.eye = (rows == cols).astype(F32)
        self.mask = rows < cols if backward else rows >= cols
        self.d = 1 if backward else 0
        self.order = list(range(SCAN_CHUNKS))[::-1] if backward else list(range(SCAN_CHUNKS))
        self.p_ref, self.v_ref, self.ed_ref, self.state_ref = p_ref, v_ref, ed_ref, state_ref
        self.states = [state_ref[h] for h in range(G_HEADS)]

    @staticmethod
    def _rows(c):
        return slice(c * G_CHUNK, (c + 1) * G_CHUNK)

    @staticmethod
    def _kcols(part, h):
        return slice(part * G_KEY_WIDTH + h * G_KEY_DIM, part * G_KEY_WIDTH + (h + 1) * G_KEY_DIM)

    @staticmethod
    def _vcols(h):
        return slice(h * G_VAL_DIM, (h + 1) * G_VAL_DIM)

    def attention(self):
        p = self.p_ref
        self.atts = {(c, h): jnp.where(self.mask,
                                       _dot_nt(p[self._rows(c), self._kcols(0, h)], p[self._rows(c), self._kcols(1, h)]),
                                       0.0).astype(BF16)
                     for c in self.order for h in range(G_HEADS)}

    def chunk(self, c):
        p, v, ed, d = self.p_ref, self.v_ref, self.ed_ref, self.d
        inters = []
        for h in range(G_HEADS):
            e_row = ed[c, 2 * d:2 * d + 1, h * G_KEY_DIM:(h + 1) * G_KEY_DIM]
            d_row = ed[c, 2 * d + 1:2 * d + 2, h * G_KEY_DIM:(h + 1) * G_KEY_DIM]
            e_col = jnp.sum(self.eye * e_row, axis=1, keepdims=True)
            d_col = jnp.sum(self.eye * d_row, axis=1, keepdims=True)
            inters.append(_dot(p[self._rows(c), self._kcols(0, h)], (self.states[h] * e_col).astype(BF16)))
            self.states[h] = (self.states[h] * d_col
                              + _dot_tn(p[self._rows(c), self._kcols(2, h)], v[self._rows(c), self._vcols(h)]))
        return [_dot(self.atts[c, h], v[self._rows(c), self._vcols(h)]) + inters[h] for h in range(G_HEADS)]

    def finish(self):
        for h in range(G_HEADS):
            self.state_ref[h] = self.states[h]


def _l1_bwd_kernel(x_ref, lr_ref, w_ref, wg_ref, bg_ref, tril_ref, triu_ref,
                   pf_ref, v_ref, g_ref, ed_ref, ob_ref,
                   pb_s, v_s, ed_s, state_s, *, n_blk, blocks_per_seq):
    t = pl.program_id(0)

    @pl.when(t == 0)
    def _():
        pb_s[...] = jnp.zeros_like(pb_s)
        v_s[...] = jnp.zeros_like(v_s)
        ed_s[...] = jnp.zeros_like(ed_s)

    @pl.when((t == 0) | (lax.rem(n_blk - t + blocks_per_seq, blocks_per_seq) == blocks_per_seq - 1))
    def _():
        state_s[...] = jnp.zeros_like(state_s)

    kw = G_KEY_WIDTH
    scan = _GlaScan(pb_s, v_s, ed_s, state_s, backward=True)
    scan.attention()
    pending_chunks = list(scan.order)
    outs = {}

    def scan_piece():
        if pending_chunks:
            c = pending_chunks.pop(0)
            outs[c] = scan.chunk(c)

    half_rows = SCAN_ROWS // 2
    half_chunks = SCAN_CHUNKS // 2
    pb_new, ed_new, v_new = {}, {}, []
    for hf in range(2):
        rs = slice(hf * half_rows, (hf + 1) * half_rows)
        xb = x_ref[rs, :].astype(BF16)
        z = _dot(lr_ref[rs, :], wg_ref[...]) + bg_ref[...]
        q = _dot(xb, w_ref[:, 0:kw]) * (G_KEY_DIM ** -0.5)
        k = _dot(xb, w_ref[:, kw:2 * kw])
        scan_piece()
        la = _log2_decay(z)
        cums = [(_split_cumsum(tril_ref[...], la[c * G_CHUNK:(c + 1) * G_CHUNK, 0:kw]),
                 _split_cumsum(triu_ref[...], la[c * G_CHUNK:(c + 1) * G_CHUNK, kw:2 * kw]))
                for c in range(half_chunks)]
        v_half = jnp.concatenate([_dot(xb, w_ref[:, 2 * kw + c * kw:2 * kw + (c + 1) * kw]).astype(BF16)
                                  for c in range(2)], axis=1)
        v_ref[rs, :] = v_half
        v_new.append(v_half)
        for cl in range(half_chunks):
            c = hf * half_chunks + cl
            r0 = c * G_CHUNK
            qc = q[cl * G_CHUNK:(cl + 1) * G_CHUNK]
            kc = k[cl * G_CHUNK:(cl + 1) * G_CHUNK]
            b, s = cums[cl]
            b_mid = b[G_MID - 1:G_MID]
            b_end = b[G_CHUNK - 1:G_CHUNK]
            pf_ref[r0:r0 + G_CHUNK, 0:kw] = (qc * jnp.exp2(b - b_mid)).astype(BF16)
            pf_ref[r0:r0 + G_CHUNK, kw:2 * kw] = (kc * jnp.exp2(b_mid - b)).astype(BF16)
            pf_ref[r0:r0 + G_CHUNK, 2 * kw:3 * kw] = (kc * jnp.exp2(b_end - b)).astype(BF16)
            s_mid = s[G_MID:G_MID + 1]
            s_end = s[0:1]
            pb_new[c] = jnp.concatenate([(qc * jnp.exp2(s - s_mid)).astype(BF16),
                                         (kc * jnp.exp2(s_mid - s)).astype(BF16),
                                         (kc * jnp.exp2(s_end - s)).astype(BF16)], axis=1)
            ed_new[c] = jnp.concatenate([jnp.exp2(b_mid), jnp.exp2(b_end), jnp.exp2(s_mid), jnp.exp2(s_end)], axis=0)
            ed_ref[c] = ed_new[c]
        for c in range(2):
            c0 = 2 * kw + G_VAL_WIDTH + c * kw
            g_ref[rs, c * kw:(c + 1) * kw] = _silu(_dot(xb, w_ref[:, c0:c0 + kw])).astype(BF16)
        scan_piece()
    while pending_chunks:
        scan_piece()
    scan.finish()
    for c, per_head in outs.items():
        for h, o in enumerate(per_head):
            ob_ref[c * G_CHUNK:(c + 1) * G_CHUNK, h * G_VAL_DIM:(h + 1) * G_VAL_DIM] = o.astype(BF16)
    for c in range(SCAN_CHUNKS):
        pb_s[c * G_CHUNK:(c + 1) * G_CHUNK, :] = pb_new[c]
        ed_s[c] = ed_new[c]
    v_s[...] = jnp.concatenate(v_new, axis=0)


def _l1_bwd(x2d, lr, w_bf16, wg, bg, tril, triu, blocks_per_seq):
    m = x2d.shape[0]
    n_blk = m // SCAN_ROWS
    n_main = w_bf16.shape[1]
    const2 = lambda t: (0, 0)
    proj = lambda t: (jnp.maximum(n_blk - 1 - t, 0), 0)
    proj3 = lambda t: (jnp.maximum(n_blk - 1 - t, 0), 0, 0)
    scanned = lambda t: (jnp.minimum(n_blk - t, n_blk - 1), 0)
    return pl.pallas_call(
        functools.partial(_l1_bwd_kernel, n_blk=n_blk, blocks_per_seq=blocks_per_seq),
        grid=(n_blk + 1,),
        in_specs=[pl.BlockSpec((SCAN_ROWS, D_MODEL), proj),
                  pl.BlockSpec((SCAN_ROWS, 128), proj),
                  pl.BlockSpec((D_MODEL, n_main), const2),
                  pl.BlockSpec((128, 2 * G_KEY_WIDTH), const2),
                  pl.BlockSpec((1, 2 * G_KEY_WIDTH), const2),
                  pl.BlockSpec((G_CHUNK, 2 * G_CHUNK), const2),
                  pl.BlockSpec((G_CHUNK, 2 * G_CHUNK), const2)],
        out_specs=[pl.BlockSpec((SCAN_ROWS, 3 * G_KEY_WIDTH), proj),
                   pl.BlockSpec((SCAN_ROWS, G_VAL_WIDTH), proj),
                   pl.BlockSpec((SCAN_ROWS, G_VAL_WIDTH), proj),
                   pl.BlockSpec((SCAN_CHUNKS, 4, G_KEY_WIDTH), proj3),
                   pl.BlockSpec((SCAN_ROWS, G_VAL_WIDTH), scanned)],
        out_shape=[jax.ShapeDtypeStruct((m, 3 * G_KEY_WIDTH), BF16),
                   jax.ShapeDtypeStruct((m, G_VAL_WIDTH), BF16),
                   jax.ShapeDtypeStruct((m, G_VAL_WIDTH), BF16),
                   jax.ShapeDtypeStruct((m // G_CHUNK, 4, G_KEY_WIDTH), F32),
                   jax.ShapeDtypeStruct((m, G_VAL_WIDTH), BF16)],
        scratch_shapes=[pltpu.VMEM((SCAN_ROWS, 3 * G_KEY_WIDTH), BF16),
                        pltpu.VMEM((SCAN_ROWS, G_VAL_WIDTH), BF16),
                        pltpu.VMEM((SCAN_CHUNKS, 4, G_KEY_WIDTH), F32),
                        pltpu.VMEM((G_HEADS, G_KEY_DIM, G_VAL_DIM), F32)],
        compiler_params=pltpu.CompilerParams(dimension_semantics=("arbitrary",),
                                             vmem_limit_bytes=VMEM_LIMIT_BYTES),
        name="l1_bwd",
    )(x2d, lr, w_bf16, wg, bg, tril, triu)


def _gla_fwd_kernel(p_ref, v_ref, ed_ref, ob_ref, g_ref, x_ref, hn_ref, wo_ref, lng_ref, lnb_ref,
                    o_ref, state_ref):
    @pl.when(pl.program_id(1) == 0)
    def _():
        state_ref[...] = jnp.zeros_like(state_ref)

    scan = _GlaScan(p_ref, v_ref, ed_ref, state_ref, backward=False)
    scan.attention()
    outs = {c: scan.chunk(c) for c in scan.order}
    scan.finish()
    acts = []
    for c in range(SCAN_CHUNKS):
        rsl = slice(c * G_CHUNK, (c + 1) * G_CHUNK)
        segs = []
        for h in range(G_HEADS):
            seg = outs[c][h] + ob_ref[rsl, h * G_VAL_DIM:(h + 1) * G_VAL_DIM].astype(F32)
            ms = jnp.mean(seg * seg, axis=-1, keepdims=True)
            segs.append(seg * lax.rsqrt(ms + RMS_EPS))
        on = jnp.concatenate(segs, axis=1) * hn_ref[...]
        acts.append((on * g_ref[rsl, :].astype(F32)).astype(BF16))
    half = SCAN_CHUNKS // 2
    for hf in range(2):
        rs = slice(hf * half * G_CHUNK, (hf + 1) * half * G_CHUNK)
        y = DN_ALPHA * x_ref[rs, :] + _dot(jnp.concatenate(acts[hf * half:(hf + 1) * half], axis=0), wo_ref[...])
        o_ref[rs, :] = _layer_norm(y, lng_ref[...], lnb_ref[...])


def _gla_fwd(p, v, ed, o_b, gate, x2d, head_norm, wo_bf16, ln_g, ln_b, n_seq, n_step):
    m = x2d.shape[0]
    row = lambda b, j: (b * n_step + j, 0)
    row3 = lambda b, j: (b * n_step + j, 0, 0)
    const2 = lambda b, j: (0, 0)
    return pl.pallas_call(
        _gla_fwd_kernel,
        grid=(n_seq, n_step),
        in_specs=[pl.BlockSpec((SCAN_ROWS, 3 * G_KEY_WIDTH), row),
                  pl.BlockSpec((SCAN_ROWS, G_VAL_WIDTH), row),
                  pl.BlockSpec((SCAN_CHUNKS, 4, G_KEY_WIDTH), row3),
                  pl.BlockSpec((SCAN_ROWS, G_VAL_WIDTH), row),
                  pl.BlockSpec((SCAN_ROWS, G_VAL_WIDTH), row),
                  pl.BlockSpec((SCAN_ROWS, D_MODEL), row),
                  pl.BlockSpec((1, G_VAL_WIDTH), const2),
                  pl.BlockSpec((G_VAL_WIDTH, D_MODEL), const2),
                  pl.BlockSpec((1, D_MODEL), const2),
                  pl.BlockSpec((1, D_MODEL), const2)],
        out_specs=pl.BlockSpec((SCAN_ROWS, D_MODEL), row),
        out_shape=jax.ShapeDtypeStruct((m, D_MODEL), F32),
        scratch_shapes=[pltpu.VMEM((G_HEADS, G_KEY_DIM, G_VAL_DIM), F32)],
        compiler_params=pltpu.CompilerParams(dimension_semantics=("arbitrary", "arbitrary"),
                                             vmem_limit_bytes=VMEM_LIMIT_BYTES),
        name="gla_fwd",
    )(p, v, ed, o_b, gate, x2d, head_norm, wo_bf16, ln_g, ln_b)


def _trunk(x, l0, l1):
    n_seq, seq, _ = x.shape
    assert seq % L0_ROWS == 0 and seq % SCAN_ROWS == 0 and SCAN_CHUNKS % 2 == 0
    x2d = x.reshape(n_seq * seq, D_MODEL)
    x1, lr = _l0_layer(x2d, l0["sink"], l0["w_in"], l0["bias"], l0["w_out"], l0["ln_g"], l0["ln_b"],
                       l1["w_lr"], seq // L0_ROWS)
    p_f, v, gate1, ed, o_b = _l1_bwd(x1, lr, l1["w_in"], l1["w_gate"], l1["b_gate"], l1["tril"], l1["triu"],
                                     seq // SCAN_ROWS)
    y = _gla_fwd(p_f, v, ed, o_b, gate1, x1, l1["head_norm"], l1["w_out"], l1["ln_g"], l1["ln_b"],
                 n_seq, seq // SCAN_ROWS)
    return y.reshape(n_seq, seq, D_MODEL)


def kernel(x_prompt, x_sample, l0_w_in, l0_sink, l0_w_out, l0_ln_g, l0_ln_b,
           l1_w_in, l1_w_gate_f, l1_b_gate_f, l1_w_gate_b, l1_b_gate_b, l1_head_norm,
           l1_w_out, l1_ln_g, l1_ln_b):
    assert x_prompt.shape[1] == x_sample.shape[1]
    row = lambda t: t.reshape(1, -1).astype(F32)
    l0 = dict(w_in=l0_w_in.astype(BF16), sink=l0_sink.astype(F32), bias=_attn_bias_table(),
              w_out=l0_w_out.astype(BF16), ln_g=row(l0_ln_g), ln_b=row(l0_ln_b))
    n_main = 2 * G_KEY_WIDTH + 2 * G_VAL_WIDTH
    w_lr = jnp.zeros((D_MODEL, 128), BF16).at[:, :2 * G_RANK].set(l1_w_in[:, n_main:].astype(BF16))
    w_gate = jnp.zeros((128, 2 * G_KEY_WIDTH), BF16)
    w_gate = w_gate.at[:G_RANK, :G_KEY_WIDTH].set(l1_w_gate_f.astype(BF16))
    w_gate = w_gate.at[G_RANK:2 * G_RANK, G_KEY_WIDTH:].set(l1_w_gate_b.astype(BF16))
    b_gate = jnp.concatenate([l1_b_gate_f, l1_b_gate_b]).reshape(1, -1).astype(F32)
    tri = np.tril(np.ones((G_CHUNK, G_CHUNK), np.float32))
    l1 = dict(w_in=l1_w_in[:, :n_main].astype(BF16), w_lr=w_lr, w_gate=w_gate, b_gate=b_gate,
              tril=jnp.asarray(np.concatenate([tri, tri], axis=1), BF16),
              triu=jnp.asarray(np.concatenate([tri.T, tri.T], axis=1), BF16),
              head_norm=row(l1_head_norm), w_out=l1_w_out.astype(BF16),
              ln_g=row(l1_ln_g), ln_b=row(l1_ln_b))
    return (_trunk(x_prompt, l0, l1), _trunk(x_sample, l0, l1))
```

```python
import functools

import jax
import jax.numpy as jnp
import numpy as np
from jax import lax
from jax.experimental import pallas as pl
from jax.experimental.pallas import tpu as pltpu

F32 = jnp.float32
BF16 = jnp.bfloat16

D_MODEL = 1024
DEPTH = 2
LN_EPS = 1e-5
RMS_EPS = 1e-6
DN_ALPHA = (2 * DEPTH) ** 0.25
NEG = -1e30
LOG2E = 1.4426950408889634

A_HEADS = 16
A_KV_HEADS = 4
A_HEAD_DIM = 64
A_REP = A_HEADS // A_KV_HEADS
A_KV_WIDTH = A_KV_HEADS * A_HEAD_DIM
A_BLOCK = 128
A_PAIRS = A_HEADS // 2
A_WIN = 3 * A_BLOCK
A_QK_AHEAD = 3

G_HEADS = 4
G_KEY_DIM = 128
G_VAL_DIM = 256
G_KEY_WIDTH = G_HEADS * G_KEY_DIM
G_VAL_WIDTH = G_HEADS * G_VAL_DIM
G_RANK = 16
G_TAU = 16.0
G_CHUNK = 128
G_MID = G_CHUNK // 2

L0_SUB = 2
L0_ROWS = L0_SUB * A_BLOCK
SCAN_CHUNKS = 4
SCAN_ROWS = SCAN_CHUNKS * G_CHUNK
V7X_VMEM_BYTES = 64 * 1024 * 1024
VMEM_LIMIT_BYTES = V7X_VMEM_BYTES * 7 // 8


def _layer_norm(y, g, b):
    mu = jnp.mean(y, axis=-1, keepdims=True)
    yc = y - mu
    var = jnp.mean(yc * yc, axis=-1, keepdims=True)
    return yc * lax.rsqrt(var + LN_EPS) * g + b


def _silu(g):
    h = 0.5 * g
    return h + h * jnp.tanh(h)


def _dot(a, b):
    return jnp.dot(a, b, preferred_element_type=F32)


def _dot_nt(a, b):
    return lax.dot_general(a, b, (((1,), (1,)), ((), ())), preferred_element_type=F32)


def _dot_tn(a, b):
    return lax.dot_general(a, b, (((0,), (0,)), ((), ())), preferred_element_type=F32)


def _l0_kernel(sink_ref, x_ref, xo_ref, w_ref, bias_a_ref, bias_b_ref, wo_ref, lng_ref, lnb_ref, wlr_ref,
               o_ref, lr_ref, q_s, g_s, kv_s, a_s):
    t = pl.program_id(0)

    @pl.when(t == 0)
    def _():
        q_s[...] = jnp.zeros_like(q_s)
        g_s[...] = jnp.zeros_like(g_s)
        kv_s[...] = jnp.zeros_like(kv_s)
        a_s[...] = jnp.zeros_like(a_s)

    new2, old2 = lax.rem(t, 2), lax.rem(t + 1, 2)
    k_new, k_cur, k_old = lax.rem(t, 3), lax.rem(t + 2, 3), lax.rem(t + 1, 3)
    q_blk = q_s[old2]
    g_blk = g_s[old2]
    kv_cur = kv_s[k_cur]
    kv_old_tail = kv_s[k_old, A_BLOCK:, :]
    a_prev = a_s[...]
    xb = x_ref[...].astype(BF16)

    zeros = jnp.zeros((A_WIN, A_HEAD_DIM), BF16)
    lane = lax.broadcasted_iota(jnp.int32, (A_BLOCK, 2 * A_HEAD_DIM), 1)
    first_half = lane < A_HEAD_DIM
    first_row_half = lax.broadcasted_iota(jnp.int32, (1, 2 * A_HEAD_DIM), 1) < A_HEAD_DIM

    ones_row = lax.broadcasted_iota(jnp.int32, (2 * A_WIN, 2 * A_HEAD_DIM), 0) < A_WIN
    ones_lane = lax.broadcasted_iota(jnp.int32, (2 * A_WIN, 2 * A_HEAD_DIM), 1) < A_HEAD_DIM
    ones_blk = (ones_row == ones_lane).astype(BF16)

    def slabs(prev, cur, nxt):
        win = jnp.concatenate([prev, cur, nxt], axis=0)
        kk, vv = [], []
        for g in range(A_KV_HEADS):
            kg = win[:, g * A_HEAD_DIM:(g + 1) * A_HEAD_DIM]
            vg = win[:, A_KV_WIDTH + g * A_HEAD_DIM:A_KV_WIDTH + (g + 1) * A_HEAD_DIM]
            kk.append(jnp.concatenate([jnp.concatenate([kg, zeros], axis=1),
                                       jnp.concatenate([zeros, kg], axis=1)], axis=0))
            vv.append(jnp.concatenate([jnp.concatenate([jnp.concatenate([vg, zeros], axis=1),
                                                        jnp.concatenate([zeros, vg], axis=1)], axis=0),
                                       ones_blk], axis=1))
        return kk, vv

    bias_refs = (bias_a_ref, bias_b_ref)
    win_slabs = [slabs(kv_old_tail, kv_cur[:A_BLOCK], kv_cur[A_BLOCK:]), None]
    n_stage = L0_SUB * A_PAIRS

    def scores(j):
        sub, i = divmod(j, A_PAIRS)
        qp = q_blk[sub * A_BLOCK:(sub + 1) * A_BLOCK, i * 128:(i + 1) * 128]
        return _dot_nt(qp, win_slabs[sub][0][i // (A_REP // 2)])

    def softmax(j, s):
        sub, i = divmod(j, A_PAIRS)
        t_ = s + bias_refs[sub][0, i]
        ps, ms = [], []
        for hh in range(2):
            th = t_[:, hh * A_WIN:(hh + 1) * A_WIN]
            m = jnp.max(th, axis=-1, keepdims=True)
            ps.append(jnp.exp2(th - m).astype(BF16))
            ms.append(m)
        sink = jnp.where(first_row_half, sink_ref[2 * i] * LOG2E, sink_ref[2 * i + 1] * LOG2E)
        return jnp.concatenate(ps, axis=1), jnp.exp2(sink - jnp.where(first_half, ms[0], ms[1]))

    def weighted_values(j, p, sink_term):
        sub, i = divmod(j, A_PAIRS)
        rows = slice(sub * A_BLOCK, (sub + 1) * A_BLOCK)
        pv = _dot(p, win_slabs[sub][1][i // (A_REP // 2)])
        o = pv[:, :128] * (1.0 / (pv[:, 128:] + sink_term))
        a_s[rows, i * 128:(i + 1) * 128] = (o * _silu(g_blk[rows, i * 128:(i + 1) * 128].astype(F32))).astype(BF16)

    y_chunks = []
    fw = D_MODEL // 4

    def out_proj_chunk(c):
        y_chunks.append(_dot(a_prev, wo_ref[:, c * fw:(c + 1) * fw]))
        if c == 3:
            y = DN_ALPHA * xo_ref[...] + jnp.concatenate(y_chunks, axis=1)
            y_chunks[:] = [_layer_norm(y, lng_ref[...], lnb_ref[...])]
            o_ref[...] = y_chunks[0]

    def low_rank_chunk(c):
        lr_ref[...] = _dot(y_chunks[0].astype(BF16), wlr_ref[...]).astype(BF16)

    def q_chunk(c):
        acc = _dot(xb, w_ref[:, c * fw:(c + 1) * fw])
        q_s[new2, :, c * fw:(c + 1) * fw] = (acc * (A_HEAD_DIM ** -0.5 * LOG2E)).astype(BF16)

    def gate_chunk(c):
        c0 = D_MODEL + 2 * A_KV_WIDTH + c * fw
        g_s[new2, :, c * fw:(c + 1) * fw] = _dot(xb, w_ref[:, c0:c0 + fw]).astype(BF16)

    kv_parts = []

    def kv_chunk(c):
        c0 = D_MODEL + c * A_KV_WIDTH
        kv_parts.append(_dot(xb, w_ref[:, c0:c0 + A_KV_WIDTH]).astype(BF16))
        if c == 1:
            kv_new = jnp.concatenate(kv_parts, axis=1)
            kv_s[k_new] = kv_new
            win_slabs[1] = slabs(kv_cur[:A_BLOCK], kv_cur[A_BLOCK:], kv_new[:A_BLOCK])

    fillers = [(kv_chunk, c) for c in range(2)] + [(out_proj_chunk, c) for c in range(4)]
    assert len(fillers) <= n_stage and A_QK_AHEAD + 2 <= A_PAIRS

    fillers += ([(q_chunk, c) for c in range(3)] + [(low_rank_chunk, 0), (q_chunk, 3)]
                + [(gate_chunk, c) for c in range(4)])
    assert len(fillers) <= n_stage

    s_ready = {j: scores(j) for j in range(A_QK_AHEAD)}
    pending = None
    for j in range(n_stage):
        if j + A_QK_AHEAD < n_stage:
            s_ready[j + A_QK_AHEAD] = scores(j + A_QK_AHEAD)
        p, sink_term = softmax(j, s_ready.pop(j))
        if pending is not None:
            weighted_values(*pending)
        pending = (j, p, sink_term)
        if j < len(fillers):
            fn, c = fillers[j]
            fn(c)
    weighted_values(*pending)


def _l0_layer(x2d, sink, w_bf16, bias, wo_bf16, ln_g, ln_b, w_lr, blocks_per_seq):
    m = x2d.shape[0]
    done = lambda t: (jnp.maximum(t - 2, 0), 0)
    n_blk = m // L0_ROWS
    n_w = w_bf16.shape[1]
    const2 = lambda t: (0, 0)
    var_a = lambda t: (jnp.where(lax.rem(t + blocks_per_seq - 1, blocks_per_seq) == 0, 0, 1), 0, 0, 0)
    var_b = lambda t: (jnp.where(lax.rem(t, blocks_per_seq) == 0, 2, 1), 0, 0, 0)
    return pl.pallas_call(
        _l0_kernel,
        grid=(n_blk + 2,),
        in_specs=[pl.BlockSpec(memory_space=pltpu.SMEM),
                  pl.BlockSpec((L0_ROWS, D_MODEL), lambda t: (jnp.minimum(t, n_blk - 1), 0)),
                  pl.BlockSpec((L0_ROWS, D_MODEL), done),
                  pl.BlockSpec((D_MODEL, n_w), const2),
                  pl.BlockSpec((1, A_PAIRS, A_BLOCK, 2 * A_WIN), var_a),
                  pl.BlockSpec((1, A_PAIRS, A_BLOCK, 2 * A_WIN), var_b),
                  pl.BlockSpec((D_MODEL, D_MODEL), const2),
                  pl.BlockSpec((1, D_MODEL), const2),
                  pl.BlockSpec((1, D_MODEL), const2),
                  pl.BlockSpec((D_MODEL, 128), const2)],
        out_specs=[pl.BlockSpec((L0_ROWS, D_MODEL), done),
                   pl.BlockSpec((L0_ROWS, 128), done)],
        out_shape=[jax.ShapeDtypeStruct((m, D_MODEL), F32),
                   jax.ShapeDtypeStruct((m, 128), BF16)],
        scratch_shapes=[pltpu.VMEM((2, L0_ROWS, D_MODEL), BF16),
                        pltpu.VMEM((2, L0_ROWS, D_MODEL), BF16),
                        pltpu.VMEM((3, L0_ROWS, 2 * A_KV_WIDTH), BF16),
                        pltpu.VMEM((L0_ROWS, D_MODEL), BF16)],
        compiler_params=pltpu.CompilerParams(dimension_semantics=("arbitrary",),
                                             vmem_limit_bytes=VMEM_LIMIT_BYTES),
        name="l0_layer",
    )(sink, x2d, x2d, w_bf16, bias, bias, wo_bf16, ln_g, ln_b, w_lr)


def _attn_bias_table():
    qi = np.arange(A_BLOCK)[:, None] + A_BLOCK
    kj = np.arange(A_WIN)[None, :]
    dist = np.abs(qi - kj).astype(np.float32)
    in_win = dist <= A_BLOCK
    slopes = (2.0 ** (-8.0 * np.arange(1, A_HEADS + 1) / A_HEADS)).astype(np.float32)
    per_head = (-slopes[:, None, None] * dist[None]).astype(np.float32) * np.float32(LOG2E)
    variants = []
    for valid in (kj >= A_BLOCK, kj >= 0, kj < 2 * A_BLOCK):
        b = np.where((in_win & valid)[None], per_head, np.float32(NEG)).astype(np.float32)
        variants.append(b.reshape(A_PAIRS, 2, A_BLOCK, A_WIN).transpose(0, 2, 1, 3).reshape(A_PAIRS, A_BLOCK, 2 * A_WIN))
    return jnp.asarray(np.stack(variants))


def _log2_decay(z):
    return (jnp.minimum(z, 0.0) - jnp.log(1.0 + jnp.exp(-jnp.abs(z)))) * (LOG2E / G_TAU)


def _split_cumsum(tri2_bf16, la):
    hi = la.astype(BF16)
    lo = (la - hi.astype(F32)).astype(BF16)
    return _dot(tri2_bf16, jnp.concatenate([hi, lo], axis=0))


class _GlaScan:
    def __init__(self, p_ref, v_ref, ed_ref, state_ref, backward):
        rows = lax.broadcasted_iota(jnp.int32, (G_CHUNK, G_CHUNK), 0)
        cols = lax.broadcasted_iota(jnp.int32, (G_CHUNK, G_CHUNK), 1)
        self.eye = (rows == cols).astype(F32)
        self.mask = rows < cols if backward else rows >= cols
        self.d = 1 if backward else 0
        self.order = list(range(SCAN_CHUNKS))[::-1] if backward else list(range(SCAN_CHUNKS))
        self.p_ref, self.v_ref, self.ed_ref, self.state_ref = p_ref, v_ref, ed_ref, state_ref
        self.states = [state_ref[h] for h in range(G_HEADS)]

    @staticmethod
    def _rows(c):
        return slice(c * G_CHUNK, (c + 1) * G_CHUNK)

    @staticmethod
    def _kcols(part, h):
        return slice(part * G_KEY_WIDTH + h * G_KEY_DIM, part * G_KEY_WIDTH + (h + 1) * G_KEY_DIM)

    @staticmethod
    def _vcols(h):
        return slice(h * G_VAL_DIM, (h + 1) * G_VAL_DIM)

    def attention(self):
        p = self.p_ref
        self.atts = {(c, h): jnp.where(self.mask,
                                       _dot_nt(p[self._rows(c), self._kcols(0, h)], p[self._rows(c), self._kcols(1, h)]),
                                       0.0).astype(BF16)
                     for c in self.order for h in range(G_HEADS)}

    def chunk(self, c):
        p, v, ed, d = self.p_ref, self.v_ref, self.ed_ref, self.d
        outs = []
        for h in range(G_HEADS):
            e_row = ed[c, 2 * d:2 * d + 1, h * G_KEY_DIM:(h + 1) * G_KEY_DIM]
            d_row = ed[c, 2 * d + 1:2 * d + 2, h * G_KEY_DIM:(h + 1) * G_KEY_DIM]
            e_col = jnp.sum(self.eye * e_row, axis=1, keepdims=True)
            d_col = jnp.sum(self.eye * d_row, axis=1, keepdims=True)
            qd = p[self._rows(c), self._kcols(0, h)]
            vh = v[self._rows(c), self._vcols(h)]
            outs.append(_dot(jnp.concatenate([self.atts[c, h], qd], axis=1),
                             jnp.concatenate([vh, (self.states[h] * e_col).astype(BF16)], axis=0)))
            self.states[h] = self.states[h] * d_col + _dot_tn(p[self._rows(c), self._kcols(2, h)], vh)
        return outs

    def finish(self):
        for h in range(G_HEADS):
            self.state_ref[h] = self.states[h]


def _l1_bwd_kernel(x_ref, lr_ref, w_ref, wg_ref, bg_ref, tril_ref, triu_ref,
                   pf_ref, v_ref, g_ref, ed_ref, ob_ref,
                   pb_s, v_s, ed_s, state_s, *, n_blk, blocks_per_seq):
    t = pl.program_id(0)

    @pl.when(t == 0)
    def _():
        pb_s[...] = jnp.zeros_like(pb_s)
        v_s[...] = jnp.zeros_like(v_s)
        ed_s[...] = jnp.zeros_like(ed_s)

    @pl.when((t == 0) | (lax.rem(n_blk - t + blocks_per_seq, blocks_per_seq) == blocks_per_seq - 1))
    def _():
        state_s[...] = jnp.zeros_like(state_s)

    kw = G_KEY_WIDTH
    scan = _GlaScan(pb_s, v_s, ed_s, state_s, backward=True)
    scan.attention()
    pending_chunks = list(scan.order)
    outs = {}

    def scan_piece():
        if pending_chunks:
            c = pending_chunks.pop(0)
            outs[c] = scan.chunk(c)

    half_rows = SCAN_ROWS // 2
    half_chunks = SCAN_CHUNKS // 2
    pb_new, ed_new, v_new = {}, {}, []
    for hf in range(2):
        rs = slice(hf * half_rows, (hf + 1) * half_rows)
        xb = x_ref[rs, :].astype(BF16)
        z = _dot(lr_ref[rs, :], wg_ref[...]) + bg_ref[...]
        q = _dot(xb, w_ref[:, 0:kw]) * (G_KEY_DIM ** -0.5)
        k = _dot(xb, w_ref[:, kw:2 * kw])
        scan_piece()
        la = _log2_decay(z)
        cums = [(_split_cumsum(tril_ref[...], la[c * G_CHUNK:(c + 1) * G_CHUNK, 0:kw]),
                 _split_cumsum(triu_ref[...], la[c * G_CHUNK:(c + 1) * G_CHUNK, kw:2 * kw]))
                for c in range(half_chunks)]
        v_half = jnp.concatenate([_dot(xb, w_ref[:, 2 * kw + c * kw:2 * kw + (c + 1) * kw]).astype(BF16)
                                  for c in range(2)], axis=1)
        v_ref[rs, :] = v_half
        v_new.append(v_half)
        for cl in range(half_chunks):
            c = hf * half_chunks + cl
            r0 = c * G_CHUNK
            qc = q[cl * G_CHUNK:(cl + 1) * G_CHUNK]
            kc = k[cl * G_CHUNK:(cl + 1) * G_CHUNK]
            b, s = cums[cl]
            b_mid = b[G_MID - 1:G_MID]
            b_end = b[G_CHUNK - 1:G_CHUNK]
            pf_ref[r0:r0 + G_CHUNK, 0:kw] = (qc * jnp.exp2(b - b_mid)).astype(BF16)
            pf_ref[r0:r0 + G_CHUNK, kw:2 * kw] = (kc * jnp.exp2(b_mid - b)).astype(BF16)
            pf_ref[r0:r0 + G_CHUNK, 2 * kw:3 * kw] = (kc * jnp.exp2(b_end - b)).astype(BF16)
            s_mid = s[G_MID:G_MID + 1]
            s_end = s[0:1]
            pb_new[c] = jnp.concatenate([(qc * jnp.exp2(s - s_mid)).astype(BF16),
                                         (kc * jnp.exp2(s_mid - s)).astype(BF16),
                                         (kc * jnp.exp2(s_end - s)).astype(BF16)], axis=1)
            ed_new[c] = jnp.concatenate([jnp.exp2(b_mid), jnp.exp2(b_end), jnp.exp2(s_mid), jnp.exp2(s_end)], axis=0)
            ed_ref[c] = ed_new[c]
        for c in range(2):
            c0 = 2 * kw + G_VAL_WIDTH + c * kw
            g_ref[rs, c * kw:(c + 1) * kw] = _silu(_dot(xb, w_ref[:, c0:c0 + kw])).astype(BF16)
        scan_piece()
    while pending_chunks:
        scan_piece()
    scan.finish()
    for c, per_head in outs.items():
        for h, o in enumerate(per_head):
            ob_ref[c * G_CHUNK:(c + 1) * G_CHUNK, h * G_VAL_DIM:(h + 1) * G_VAL_DIM] = o.astype(BF16)
    for c in range(SCAN_CHUNKS):
        pb_s[c * G_CHUNK:(c + 1) * G_CHUNK, :] = pb_new[c]
        ed_s[c] = ed_new[c]
    v_s[...] = jnp.concatenate(v_new, axis=0)


def _l1_bwd(x2d, lr, w_bf16, wg, bg, tril, triu, blocks_per_seq):
    m = x2d.shape[0]
    n_blk = m // SCAN_ROWS
    n_main = w_bf16.shape[1]
    const2 = lambda t: (0, 0)
    proj = lambda t: (jnp.maximum(n_blk - 1 - t, 0), 0)
    proj3 = lambda t: (jnp.maximum(n_blk - 1 - t, 0), 0, 0)
    scanned = lambda t: (jnp.minimum(n_blk - t, n_blk - 1), 0)
    return pl.pallas_call(
        functools.partial(_l1_bwd_kernel, n_blk=n_blk, blocks_per_seq=blocks_per_seq),
        grid=(n_blk + 1,),
        in_specs=[pl.BlockSpec((SCAN_ROWS, D_MODEL), proj),
                  pl.BlockSpec((SCAN_ROWS, 128), proj),
                  pl.BlockSpec((D_MODEL, n_main), const2),
                  pl.BlockSpec((128, 2 * G_KEY_WIDTH), const2),
                  pl.BlockSpec((1, 2 * G_KEY_WIDTH), const2),
                  pl.BlockSpec((G_CHUNK, 2 * G_CHUNK), const2),
                  pl.BlockSpec((G_CHUNK, 2 * G_CHUNK), const2)],
        out_specs=[pl.BlockSpec((SCAN_ROWS, 3 * G_KEY_WIDTH), proj),
                   pl.BlockSpec((SCAN_ROWS, G_VAL_WIDTH), proj),
                   pl.BlockSpec((SCAN_ROWS, G_VAL_WIDTH), proj),
                   pl.BlockSpec((SCAN_CHUNKS, 4, G_KEY_WIDTH), proj3),
                   pl.BlockSpec((SCAN_ROWS, G_VAL_WIDTH), scanned)],
        out_shape=[jax.ShapeDtypeStruct((m, 3 * G_KEY_WIDTH), BF16),
                   jax.ShapeDtypeStruct((m, G_VAL_WIDTH), BF16),
                   jax.ShapeDtypeStruct((m, G_VAL_WIDTH), BF16),
                   jax.ShapeDtypeStruct((m // G_CHUNK, 4, G_KEY_WIDTH), F32),
                   jax.ShapeDtypeStruct((m, G_VAL_WIDTH), BF16)],
        scratch_shapes=[pltpu.VMEM((SCAN_ROWS, 3 * G_KEY_WIDTH), BF16),
                        pltpu.VMEM((SCAN_ROWS, G_VAL_WIDTH), BF16),
                        pltpu.VMEM((SCAN_CHUNKS, 4, G_KEY_WIDTH), F32),
                        pltpu.VMEM((G_HEADS, G_KEY_DIM, G_VAL_DIM), F32)],
        compiler_params=pltpu.CompilerParams(dimension_semantics=("arbitrary",),
                                             vmem_limit_bytes=VMEM_LIMIT_BYTES),
        name="l1_bwd",
    )(x2d, lr, w_bf16, wg, bg, tril, triu)


def _gla_fwd_kernel(p_ref, v_ref, ed_ref, ob_ref, g_ref, x_ref, hn_ref, wo_ref, lng_ref, lnb_ref,
                    o_ref, state_ref):
    @pl.when(pl.program_id(1) == 0)
    def _():
        state_ref[...] = jnp.zeros_like(state_ref)

    scan = _GlaScan(p_ref, v_ref, ed_ref, state_ref, backward=False)
    scan.attention()
    outs = {c: scan.chunk(c) for c in scan.order}
    scan.finish()
    acts = []
    for c in range(SCAN_CHUNKS):
        rsl = slice(c * G_CHUNK, (c + 1) * G_CHUNK)
        segs = []
        for h in range(G_HEADS):
            seg = outs[c][h] + ob_ref[rsl, h * G_VAL_DIM:(h + 1) * G_VAL_DIM].astype(F32)
            ms = jnp.mean(seg * seg, axis=-1, keepdims=True)
            segs.append(seg * lax.rsqrt(ms + RMS_EPS))
        on = jnp.concatenate(segs, axis=1) * hn_ref[...]
        acts.append((on * g_ref[rsl, :].astype(F32)).astype(BF16))
    half = SCAN_CHUNKS // 2
    for hf in range(2):
        rs = slice(hf * half * G_CHUNK, (hf + 1) * half * G_CHUNK)
        y = DN_ALPHA * x_ref[rs, :] + _dot(jnp.concatenate(acts[hf * half:(hf + 1) * half], axis=0), wo_ref[...])
        o_ref[rs, :] = _layer_norm(y, lng_ref[...], lnb_ref[...])


def _gla_fwd(p, v, ed, o_b, gate, x2d, head_norm, wo_bf16, ln_g, ln_b, n_seq, n_step):
    m = x2d.shape[0]
    row = lambda b, j: (b * n_step + j, 0)
    row3 = lambda b, j: (b * n_step + j, 0, 0)
    const2 = lambda b, j: (0, 0)
    return pl.pallas_call(
        _gla_fwd_kernel,
        grid=(n_seq, n_step),
        in_specs=[pl.BlockSpec((SCAN_ROWS, 3 * G_KEY_WIDTH), row),
                  pl.BlockSpec((SCAN_ROWS, G_VAL_WIDTH), row),
                  pl.BlockSpec((SCAN_CHUNKS, 4, G_KEY_WIDTH), row3),
                  pl.BlockSpec((SCAN_ROWS, G_VAL_WIDTH), row),
                  pl.BlockSpec((SCAN_ROWS, G_VAL_WIDTH), row),
                  pl.BlockSpec((SCAN_ROWS, D_MODEL), row),
                  pl.BlockSpec((1, G_VAL_WIDTH), const2),
                  pl.BlockSpec((G_VAL_WIDTH, D_MODEL), const2),
                  pl.BlockSpec((1, D_MODEL), const2),
                  pl.BlockSpec((1, D_MODEL), const2)],
        out_specs=pl.BlockSpec((SCAN_ROWS, D_MODEL), row),
        out_shape=jax.ShapeDtypeStruct((m, D_MODEL), F32),
        scratch_shapes=[pltpu.VMEM((G_HEADS, G_KEY_DIM, G_VAL_DIM), F32)],
        compiler_params=pltpu.CompilerParams(dimension_semantics=("arbitrary", "arbitrary"),
                                             vmem_limit_bytes=VMEM_LIMIT_BYTES),
        name="gla_fwd",
    )(p, v, ed, o_b, gate, x2d, head_norm, wo_bf16, ln_g, ln_b)


def _trunk(x, l0, l1):
    n_seq, seq, _ = x.shape
    assert seq % L0_ROWS == 0 and seq % SCAN_ROWS == 0 and SCAN_CHUNKS % 2 == 0
    x2d = x.reshape(n_seq * seq, D_MODEL)
    x1, lr = _l0_layer(x2d, l0["sink"], l0["w_in"], l0["bias"], l0["w_out"], l0["ln_g"], l0["ln_b"],
                       l1["w_lr"], seq // L0_ROWS)
    p_f, v, gate1, ed, o_b = _l1_bwd(x1, lr, l1["w_in"], l1["w_gate"], l1["b_gate"], l1["tril"], l1["triu"],
                                     seq // SCAN_ROWS)
    y = _gla_fwd(p_f, v, ed, o_b, gate1, x1, l1["head_norm"], l1["w_out"], l1["ln_g"], l1["ln_b"],
                 n_seq, seq // SCAN_ROWS)
    return y.reshape(n_seq, seq, D_MODEL)


def kernel(x_prompt, x_sample, l0_w_in, l0_sink, l0_w_out, l0_ln_g, l0_ln_b,
           l1_w_in, l1_w_gate_f, l1_b_gate_f, l1_w_gate_b, l1_b_gate_b, l1_head_norm,
           l1_w_out, l1_ln_g, l1_ln_b):
    assert x_prompt.shape[1] == x_sample.shape[1]
    row = lambda t: t.reshape(1, -1).astype(F32)
    l0 = dict(w_in=l0_w_in.astype(BF16), sink=l0_sink.astype(F32), bias=_attn_bias_table(),
              w_out=l0_w_out.astype(BF16), ln_g=row(l0_ln_g), ln_b=row(l0_ln_b))
    n_main = 2 * G_KEY_WIDTH + 2 * G_VAL_WIDTH
    w_lr = jnp.zeros((D_MODEL, 128), BF16).at[:, :2 * G_RANK].set(l1_w_in[:, n_main:].astype(BF16))
    w_gate = jnp.zeros((128, 2 * G_KEY_WIDTH), BF16)
    w_gate = w_gate.at[:G_RANK, :G_KEY_WIDTH].set(l1_w_gate_f.astype(BF16))
    w_gate = w_gate.at[G_RANK:2 * G_RANK, G_KEY_WIDTH:].set(l1_w_gate_b.astype(BF16))
    b_gate = jnp.concatenate([l1_b_gate_f, l1_b_gate_b]).reshape(1, -1).astype(F32)
    tri = np.tril(np.ones((G_CHUNK, G_CHUNK), np.float32))
    l1 = dict(w_in=l1_w_in[:, :n_main].astype(BF16), w_lr=w_lr, w_gate=w_gate, b_gate=b_gate,
              tril=jnp.asarray(np.concatenate([tri, tri], axis=1), BF16),
              triu=jnp.asarray(np.concatenate([tri.T, tri.T], axis=1), BF16),
              head_norm=row(l1_head_norm), w_out=l1_w_out.astype(BF16),
              ln_g=row(l1_ln_g), ln_b=row(l1_ln_b))
    return (_trunk(x_prompt, l0, l1), _trunk(x_sample, l0, l1))
```

```python
import functools

import jax
import jax.numpy as jnp
import numpy as np
from jax import lax
from jax.experimental import pallas as pl
from jax.experimental.pallas import tpu as pltpu

F32 = jnp.float32
BF16 = jnp.bfloat16

D_MODEL = 1024
DEPTH = 2
LN_EPS = 1e-5
RMS_EPS = 1e-6
DN_ALPHA = (2 * DEPTH) ** 0.25
NEG = -1e30
LOG2E = 1.4426950408889634

A_HEADS = 16
A_KV_HEADS = 4
A_HEAD_DIM = 64
A_REP = A_HEADS // A_KV_HEADS
A_KV_WIDTH = A_KV_HEADS * A_HEAD_DIM
A_BLOCK = 128
A_PAIRS = A_HEADS // 2
A_WIN = 3 * A_BLOCK
A_QK_AHEAD = 3

G_HEADS = 4
G_KEY_DIM = 128
G_VAL_DIM = 256
G_KEY_WIDTH = G_HEADS * G_KEY_DIM
G_VAL_WIDTH = G_HEADS * G_VAL_DIM
G_RANK = 16
G_TAU = 16.0
G_CHUNK = 128
G_MID = G_CHUNK // 2

L0_SUB = 2
L0_ROWS = L0_SUB * A_BLOCK
SCAN_CHUNKS = 4
SCAN_ROWS = SCAN_CHUNKS * G_CHUNK
V7X_VMEM_BYTES = 64 * 1024 * 1024
VMEM_LIMIT_BYTES = V7X_VMEM_BYTES * 7 // 8


def _layer_norm(y, g, b):
    mu = jnp.mean(y, axis=-1, keepdims=True)
    yc = y - mu
    var = jnp.mean(yc * yc, axis=-1, keepdims=True)
    return yc * lax.rsqrt(var + LN_EPS) * g + b


def _silu(g):
    h = 0.5 * g
    return h + h * jnp.tanh(h)


def _dot(a, b):
    return jnp.dot(a, b, preferred_element_type=F32)


def _dot_nt(a, b):
    return lax.dot_general(a, b, (((1,), (1,)), ((), ())), preferred_element_type=F32)


def _dot_tn(a, b):
    return lax.dot_general(a, b, (((0,), (0,)), ((), ())), preferred_element_type=F32)


def _l0_kernel(sink_ref, x_ref, xo_ref, w_ref, bias_a_ref, bias_b_ref, wo_ref, lng_ref, lnb_ref, wlr_ref,
               o_ref, lr_ref, q_s, g_s, kv_s, a_s):
    t = pl.program_id(0)

    @pl.when(t == 0)
    def _():
        q_s[...] = jnp.zeros_like(q_s)
        g_s[...] = jnp.zeros_like(g_s)
        kv_s[...] = jnp.zeros_like(kv_s)
        a_s[...] = jnp.zeros_like(a_s)

    new2, old2 = lax.rem(t, 2), lax.rem(t + 1, 2)
    k_new, k_cur, k_old = lax.rem(t, 3), lax.rem(t + 2, 3), lax.rem(t + 1, 3)
    q_blk = q_s[old2]
    g_blk = g_s[old2]
    kv_cur = kv_s[k_cur]
    kv_old_tail = kv_s[k_old, A_BLOCK:, :]
    a_prev = a_s[...]
    xb = x_ref[...].astype(BF16)

    zeros = jnp.zeros((A_WIN, A_HEAD_DIM), BF16)
    lane = lax.broadcasted_iota(jnp.int32, (A_BLOCK, 2 * A_HEAD_DIM), 1)
    first_half = lane < A_HEAD_DIM
    first_row_half = lax.broadcasted_iota(jnp.int32, (1, 2 * A_HEAD_DIM), 1) < A_HEAD_DIM

    ones_row = lax.broadcasted_iota(jnp.int32, (2 * A_WIN, 2 * A_HEAD_DIM), 0) < A_WIN
    ones_lane = lax.broadcasted_iota(jnp.int32, (2 * A_WIN, 2 * A_HEAD_DIM), 1) < A_HEAD_DIM
    ones_blk = (ones_row == ones_lane).astype(BF16)

    def slabs(prev, cur, nxt):
        win = jnp.concatenate([prev, cur, nxt], axis=0)
        kk, vv = [], []
        for g in range(A_KV_HEADS):
            kg = win[:, g * A_HEAD_DIM:(g + 1) * A_HEAD_DIM]
            vg = win[:, A_KV_WIDTH + g * A_HEAD_DIM:A_KV_WIDTH + (g + 1) * A_HEAD_DIM]
            kk.append(jnp.concatenate([jnp.concatenate([kg, zeros], axis=1),
                                       jnp.concatenate([zeros, kg], axis=1)], axis=0))
            vv.append(jnp.concatenate([jnp.concatenate([jnp.concatenate([vg, zeros], axis=1),
                                                        jnp.concatenate([zeros, vg], axis=1)], axis=0),
                                       ones_blk], axis=1))
        return kk, vv

    bias_refs = (bias_a_ref, bias_b_ref)
    win_slabs = [slabs(kv_old_tail, kv_cur[:A_BLOCK], kv_cur[A_BLOCK:]), None]
    n_stage = L0_SUB * A_PAIRS

    def scores(j):
        sub, i = divmod(j, A_PAIRS)
        qp = q_blk[sub * A_BLOCK:(sub + 1) * A_BLOCK, i * 128:(i + 1) * 128]
        return _dot_nt(qp, win_slabs[sub][0][i // (A_REP // 2)])

    def softmax(j, s):
        sub, i = divmod(j, A_PAIRS)
        t_ = s + bias_refs[sub][0, i]
        ps, ms = [], []
        for hh in range(2):
            th = t_[:, hh * A_WIN:(hh + 1) * A_WIN]
            m = jnp.max(th, axis=-1, keepdims=True)
            ps.append(jnp.exp2(th - m).astype(BF16))
            ms.append(m)
        sink = jnp.where(first_row_half, sink_ref[2 * i] * LOG2E, sink_ref[2 * i + 1] * LOG2E)
        return jnp.concatenate(ps, axis=1), jnp.exp2(sink - jnp.where(first_half, ms[0], ms[1]))

    def weighted_values(j, p, sink_term):
        sub, i = divmod(j, A_PAIRS)
        rows = slice(sub * A_BLOCK, (sub + 1) * A_BLOCK)
        pv = _dot(p, win_slabs[sub][1][i // (A_REP // 2)])
        o = pv[:, :128] * (1.0 / (pv[:, 128:] + sink_term))
        a_s[rows, i * 128:(i + 1) * 128] = (o * _silu(g_blk[rows, i * 128:(i + 1) * 128].astype(F32))).astype(BF16)

    y_chunks = []
    fw = D_MODEL // 4

    def out_proj_chunk(c):
        y_chunks.append(_dot(a_prev, wo_ref[:, c * fw:(c + 1) * fw]))
        if c == 3:
            y = DN_ALPHA * xo_ref[...] + jnp.concatenate(y_chunks, axis=1)
            y_chunks[:] = [_layer_norm(y, lng_ref[...], lnb_ref[...])]
            o_ref[...] = y_chunks[0]

    def low_rank_chunk(c):
        lr_ref[...] = _dot(y_chunks[0].astype(BF16), wlr_ref[...]).astype(BF16)

    def q_chunk(c):
        acc = _dot(xb, w_ref[:, c * fw:(c + 1) * fw])
        q_s[new2, :, c * fw:(c + 1) * fw] = (acc * (A_HEAD_DIM ** -0.5 * LOG2E)).astype(BF16)

    def gate_chunk(c):
        c0 = D_MODEL + 2 * A_KV_WIDTH + c * fw
        g_s[new2, :, c * fw:(c + 1) * fw] = _dot(xb, w_ref[:, c0:c0 + fw]).astype(BF16)

    kv_parts = []

    def kv_chunk(c):
        c0 = D_MODEL + c * A_KV_WIDTH
        kv_parts.append(_dot(xb, w_ref[:, c0:c0 + A_KV_WIDTH]).astype(BF16))
        if c == 1:
            kv_new = jnp.concatenate(kv_parts, axis=1)
            kv_s[k_new] = kv_new
            win_slabs[1] = slabs(kv_cur[:A_BLOCK], kv_cur[A_BLOCK:], kv_new[:A_BLOCK])

    fillers = [(kv_chunk, c) for c in range(2)] + [(out_proj_chunk, c) for c in range(4)]
    assert len(fillers) <= n_stage and A_QK_AHEAD + 2 <= A_PAIRS

    fillers += ([(q_chunk, c) for c in range(3)] + [(low_rank_chunk, 0), (q_chunk, 3)]
                + [(gate_chunk, c) for c in range(4)])
    assert len(fillers) <= n_stage

    s_ready = {j: scores(j) for j in range(A_QK_AHEAD)}
    pending = None
    for j in range(n_stage):
        if j + A_QK_AHEAD < n_stage:
            s_ready[j + A_QK_AHEAD] = scores(j + A_QK_AHEAD)
        p, sink_term = softmax(j, s_ready.pop(j))
        if pending is not None:
            weighted_values(*pending)
        pending = (j, p, sink_term)
        if j < len(fillers):
            fn, c = fillers[j]
            fn(c)
    weighted_values(*pending)


def _l0_layer(x2d, sink, w_bf16, bias, wo_bf16, ln_g, ln_b, w_lr, blocks_per_seq):
    m = x2d.shape[0]
    done = lambda t: (jnp.maximum(t - 2, 0), 0)
    n_blk = m // L0_ROWS
    n_w = w_bf16.shape[1]
    const2 = lambda t: (0, 0)
    var_a = lambda t: (jnp.where(lax.rem(t + blocks_per_seq - 1, blocks_per_seq) == 0, 0, 1), 0, 0, 0)
    var_b = lambda t: (jnp.where(lax.rem(t, blocks_per_seq) == 0, 2, 1), 0, 0, 0)
    return pl.pallas_call(
        _l0_kernel,
        grid=(n_blk + 2,),
        in_specs=[pl.BlockSpec(memory_space=pltpu.SMEM),
                  pl.BlockSpec((L0_ROWS, D_MODEL), lambda t: (jnp.minimum(t, n_blk - 1), 0)),
                  pl.BlockSpec((L0_ROWS, D_MODEL), done),
                  pl.BlockSpec((D_MODEL, n_w), const2),
                  pl.BlockSpec((1, A_PAIRS, A_BLOCK, 2 * A_WIN), var_a),
                  pl.BlockSpec((1, A_PAIRS, A_BLOCK, 2 * A_WIN), var_b),
                  pl.BlockSpec((D_MODEL, D_MODEL), const2),
                  pl.BlockSpec((1, D_MODEL), const2),
                  pl.BlockSpec((1, D_MODEL), const2),
                  pl.BlockSpec((D_MODEL, 128), const2)],
        out_specs=[pl.BlockSpec((L0_ROWS, D_MODEL), done),
                   pl.BlockSpec((L0_ROWS, 128), done)],
        out_shape=[jax.ShapeDtypeStruct((m, D_MODEL), F32),
                   jax.ShapeDtypeStruct((m, 128), BF16)],
        scratch_shapes=[pltpu.VMEM((2, L0_ROWS, D_MODEL), BF16),
                        pltpu.VMEM((2, L0_ROWS, D_MODEL), BF16),
                        pltpu.VMEM((3, L0_ROWS, 2 * A_KV_WIDTH), BF16),
                        pltpu.VMEM((L0_ROWS, D_MODEL), BF16)],
        compiler_params=pltpu.CompilerParams(dimension_semantics=("arbitrary",),
                                             vmem_limit_bytes=VMEM_LIMIT_BYTES),
        name="l0_layer",
    )(sink, x2d, x2d, w_bf16, bias, bias, wo_bf16, ln_g, ln_b, w_lr)


def _attn_bias_table():
    qi = np.arange(A_BLOCK)[:, None] + A_BLOCK
    kj = np.arange(A_WIN)[None, :]
    dist = np.abs(qi - kj).astype(np.float32)
    in_win = dist <= A_BLOCK
    slopes = (2.0 ** (-8.0 * np.arange(1, A_HEADS + 1) / A_HEADS)).astype(np.float32)
    per_head = (-slopes[:, None, None] * dist[None]).astype(np.float32) * np.float32(LOG2E)
    variants = []
    for valid in (kj >= A_BLOCK, kj >= 0, kj < 2 * A_BLOCK):
        b = np.where((in_win & valid)[None], per_head, np.float32(NEG)).astype(np.float32)
        variants.append(b.reshape(A_PAIRS, 2, A_BLOCK, A_WIN).transpose(0, 2, 1, 3).reshape(A_PAIRS, A_BLOCK, 2 * A_WIN))
    return jnp.asarray(np.stack(variants))


def _log2_decay(z):
    return (jnp.minimum(z, 0.0) - jnp.log(1.0 + jnp.exp(-jnp.abs(z)))) * (LOG2E / G_TAU)


def _split_cumsum(tri2_bf16, la):
    hi = la.astype(BF16)
    lo = (la - hi.astype(F32)).astype(BF16)
    return _dot(tri2_bf16, jnp.concatenate([hi, lo], axis=0))


class _GlaScan:
    def __init__(self, p_ref, v_ref, ed_ref, state_ref, backward):
        rows = lax.broadcasted_iota(jnp.int32, (G_CHUNK, G_CHUNK), 0)
        cols = lax.broadcasted_iota(jnp.int32, (G_CHUNK, G_CHUNK), 1)
        self.eye = (rows == cols).astype(F32)
        self.mask = rows < cols if backward else rows >= cols
        self.d = 1 if backward else 0
        self.order = list(range(SCAN_CHUNKS))[::-1] if backward else list(range(SCAN_CHUNKS))
        self.p_ref, self.v_ref, self.ed_ref, self.state_ref = p_ref, v_ref, ed_ref, state_ref
        self.states = [state_ref[h] for h in range(G_HEADS)]

    @staticmethod
    def _rows(c):
        return slice(c * G_CHUNK, (c + 1) * G_CHUNK)

    @staticmethod
    def _kcols(part, h):
        return slice(part * G_KEY_WIDTH + h * G_KEY_DIM, part * G_KEY_WIDTH + (h + 1) * G_KEY_DIM)

    @staticmethod
    def _vcols(h):
        return slice(h * G_VAL_DIM, (h + 1) * G_VAL_DIM)

    def attention(self):
        p = self.p_ref
        self.atts = {(c, h): jnp.where(self.mask,
                                       _dot_nt(p[self._rows(c), self._kcols(0, h)], p[self._rows(c), self._kcols(1, h)]),
                                       0.0).astype(BF16)
                     for c in self.order for h in range(G_HEADS)}

    def chunk(self, c):
        p, v, ed, d = self.p_ref, self.v_ref, self.ed_ref, self.d
        inters = []
        for h in range(G_HEADS):
            e_row = ed[c, 2 * d:2 * d + 1, h * G_KEY_DIM:(h + 1) * G_KEY_DIM]
            d_row = ed[c, 2 * d + 1:2 * d + 2, h * G_KEY_DIM:(h + 1) * G_KEY_DIM]
            e_col = jnp.sum(self.eye * e_row, axis=1, keepdims=True)
            d_col = jnp.sum(self.eye * d_row, axis=1, keepdims=True)
            inters.append(_dot(p[self._rows(c), self._kcols(0, h)], (self.states[h] * e_col).astype(BF16)))
            self.states[h] = (self.states[h] * d_col
                              + _dot_tn(p[self._rows(c), self._kcols(2, h)], v[self._rows(c), self._vcols(h)]))
        return [_dot(self.atts[c, h], v[self._rows(c), self._vcols(h)]) + inters[h] for h in range(G_HEADS)]

    def finish(self):
        for h in range(G_HEADS):
            self.state_ref[h] = self.states[h]


def _l1_bwd_kernel(x_ref, lr_ref, w_ref, wg_ref, bg_ref, tril_ref, triu_ref,
                   pf_ref, v_ref, g_ref, ed_ref, ob_ref,
                   pb_s, v_s, ed_s, state_s, *, n_blk, blocks_per_seq):
    t = pl.program_id(0)

    @pl.when(t == 0)
    def _():
        pb_s[...] = jnp.zeros_like(pb_s)
        v_s[...] = jnp.zeros_like(v_s)
        ed_s[...] = jnp.zeros_like(ed_s)

    @pl.when((t == 0) | (lax.rem(n_blk - t + blocks_per_seq, blocks_per_seq) == blocks_per_seq - 1))
    def _():
        state_s[...] = jnp.zeros_like(state_s)

    kw = G_KEY_WIDTH
    scan = _GlaScan(pb_s, v_s, ed_s, state_s, backward=True)
    scan.attention()
    pending_chunks = list(scan.order)
    outs = {}

    def scan_piece():
        if pending_chunks:
            c = pending_chunks.pop(0)
            outs[c] = scan.chunk(c)

    half_rows = SCAN_ROWS // 2
    half_chunks = SCAN_CHUNKS // 2
    pb_new, ed_new, v_new = {}, {}, []
    for hf in range(2):
        rs = slice(hf * half_rows, (hf + 1) * half_rows)
        xb = x_ref[rs, :].astype(BF16)
        z = _dot(lr_ref[rs, :], wg_ref[...]) + bg_ref[...]
        q = _dot(xb, w_ref[:, 0:kw]) * (G_KEY_DIM ** -0.5)
        k = _dot(xb, w_ref[:, kw:2 * kw])
        scan_piece()
        la = _log2_decay(z)
        cums = [(_split_cumsum(tril_ref[...], la[c * G_CHUNK:(c + 1) * G_CHUNK, 0:kw]),
                 _split_cumsum(triu_ref[...], la[c * G_CHUNK:(c + 1) * G_CHUNK, kw:2 * kw]))
                for c in range(half_chunks)]
        v_half = jnp.concatenate([_dot(xb, w_ref[:, 2 * kw + c * kw:2 * kw + (c + 1) * kw]).astype(BF16)
                                  for c in range(2)], axis=1)
        v_ref[rs, :] = v_half
        v_new.append(v_half)
        for cl in range(half_chunks):
            c = hf * half_chunks + cl
            r0 = c * G_CHUNK
            qc = q[cl * G_CHUNK:(cl + 1) * G_CHUNK]
            kc = k[cl * G_CHUNK:(cl + 1) * G_CHUNK]
            b, s = cums[cl]
            b_mid = b[G_MID - 1:G_MID]
            b_end = b[G_CHUNK - 1:G_CHUNK]
            pf_ref[r0:r0 + G_CHUNK, 0:kw] = (qc * jnp.exp2(b - b_mid)).astype(BF16)
            pf_ref[r0:r0 + G_CHUNK, kw:2 * kw] = (kc * jnp.exp2(b_mid - b)).astype(BF16)
            pf_ref[r0:r0 + G_CHUNK, 2 * kw:3 * kw] = (kc * jnp.exp2(b_end - b)).astype(BF16)
            s_mid = s[G_MID:G_MID + 1]
            s_end = s[0:1]
            pb_new[c] = jnp.concatenate([(qc * jnp.exp2(s - s_mid)).astype(BF16),
                                         (kc * jnp.exp2(s_mid - s)).astype(BF16),
                                         (kc * jnp.exp2(s_end - s)).astype(BF16)], axis=1)
            ed_new[c] = jnp.concatenate([jnp.exp2(b_mid), jnp.exp2(b_end), jnp.exp2(s_mid), jnp.exp2(s_end)], axis=0)
            ed_ref[c] = ed_new[c]
        for c in range(2):
            c0 = 2 * kw + G_VAL_WIDTH + c * kw
            g_ref[rs, c * kw:(c + 1) * kw] = _silu(_dot(xb, w_ref[:, c0:c0 + kw])).astype(BF16)
        scan_piece()
    while pending_chunks:
        scan_piece()
    scan.finish()
    for c, per_head in outs.items():
        for h, o in enumerate(per_head):
            ob_ref[c * G_CHUNK:(c + 1) * G_CHUNK, h * G_VAL_DIM:(h + 1) * G_VAL_DIM] = o.astype(BF16)
    for c in range(SCAN_CHUNKS):
        pb_s[c * G_CHUNK:(c + 1) * G_CHUNK, :] = pb_new[c]
        ed_s[c] = ed_new[c]
    v_s[...] = jnp.concatenate(v_new, axis=0)


def _l1_bwd(x2d, lr, w_bf16, wg, bg, tril, triu, blocks_per_seq):
    m = x2d.shape[0]
    n_blk = m // SCAN_ROWS
    n_main = w_bf16.shape[1]
    const2 = lambda t: (0, 0)
    proj = lambda t: (jnp.maximum(n_blk - 1 - t, 0), 0)
    proj3 = lambda t: (jnp.maximum(n_blk - 1 - t, 0), 0, 0)
    scanned = lambda t: (jnp.minimum(n_blk - t, n_blk - 1), 0)
    return pl.pallas_call(
        functools.partial(_l1_bwd_kernel, n_blk=n_blk, blocks_per_seq=blocks_per_seq),
        grid=(n_blk + 1,),
        in_specs=[pl.BlockSpec((SCAN_ROWS, D_MODEL), proj),
                  pl.BlockSpec((SCAN_ROWS, 128), proj),
                  pl.BlockSpec((D_MODEL, n_main), const2),
                  pl.BlockSpec((128, 2 * G_KEY_WIDTH), const2),
                  pl.BlockSpec((1, 2 * G_KEY_WIDTH), const2),
                  pl.BlockSpec((G_CHUNK, 2 * G_CHUNK), const2),
                  pl.BlockSpec((G_CHUNK, 2 * G_CHUNK), const2)],
        out_specs=[pl.BlockSpec((SCAN_ROWS, 3 * G_KEY_WIDTH), proj),
                   pl.BlockSpec((SCAN_ROWS, G_VAL_WIDTH), proj),
                   pl.BlockSpec((SCAN_ROWS, G_VAL_WIDTH), proj),
                   pl.BlockSpec((SCAN_CHUNKS, 4, G_KEY_WIDTH), proj3),
                   pl.BlockSpec((SCAN_ROWS, G_VAL_WIDTH), scanned)],
        out_shape=[jax.ShapeDtypeStruct((m, 3 * G_KEY_WIDTH), BF16),
                   jax.ShapeDtypeStruct((m, G_VAL_WIDTH), BF16),
                   jax.ShapeDtypeStruct((m, G_VAL_WIDTH), BF16),
                   jax.ShapeDtypeStruct((m // G_CHUNK, 4, G_KEY_WIDTH), F32),
                   jax.ShapeDtypeStruct((m, G_VAL_WIDTH), BF16)],
        scratch_shapes=[pltpu.VMEM((SCAN_ROWS, 3 * G_KEY_WIDTH), BF16),
                        pltpu.VMEM((SCAN_ROWS, G_VAL_WIDTH), BF16),
                        pltpu.VMEM((SCAN_CHUNKS, 4, G_KEY_WIDTH), F32),
                        pltpu.VMEM((G_HEADS, G_KEY_DIM, G_VAL_DIM), F32)],
        compiler_params=pltpu.CompilerParams(dimension_semantics=("arbitrary",),
                                             vmem_limit_bytes=VMEM_LIMIT_BYTES),
        name="l1_bwd",
    )(x2d, lr, w_bf16, wg, bg, tril, triu)


def _gla_fwd_kernel(p_ref, v_ref, ed_ref, ob_ref, g_ref, x_ref, hn_ref, wo_ref, lng_ref, lnb_ref,
                    o_ref, state_ref):
    @pl.when(pl.program_id(1) == 0)
    def _():
        state_ref[...] = jnp.zeros_like(state_ref)

    scan = _GlaScan(p_ref, v_ref, ed_ref, state_ref, backward=False)
    scan.attention()
    outs = {c: scan.chunk(c) for c in scan.order}
    scan.finish()
    acts = []
    for c in range(SCAN_CHUNKS):
        rsl = slice(c * G_CHUNK, (c + 1) * G_CHUNK)
        segs = []
        for h in range(G_HEADS):
            seg = outs[c][h] + ob_ref[rsl, h * G_VAL_DIM:(h + 1) * G_VAL_DIM].astype(F32)
            ms = jnp.mean(seg * seg, axis=-1, keepdims=True)
            segs.append(seg * lax.rsqrt(ms + RMS_EPS))
        on = jnp.concatenate(segs, axis=1) * hn_ref[...]
        acts.append((on * g_ref[rsl, :].astype(F32)).astype(BF16))
    half = SCAN_CHUNKS // 2
    for hf in range(2):
        rs = slice(hf * half * G_CHUNK, (hf + 1) * half * G_CHUNK)
        y = DN_ALPHA * x_ref[rs, :] + _dot(jnp.concatenate(acts[hf * half:(hf + 1) * half], axis=0), wo_ref[...])
        o_ref[rs, :] = _layer_norm(y, lng_ref[...], lnb_ref[...])


def _gla_fwd(p, v, ed, o_b, gate, x2d, head_norm, wo_bf16, ln_g, ln_b, n_seq, n_step):
    m = x2d.shape[0]
    row = lambda b, j: (b * n_step + j, 0)
    row3 = lambda b, j: (b * n_step + j, 0, 0)
    const2 = lambda b, j: (0, 0)
    return pl.pallas_call(
        _gla_fwd_kernel,
        grid=(n_seq, n_step),
        in_specs=[pl.BlockSpec((SCAN_ROWS, 3 * G_KEY_WIDTH), row),
                  pl.BlockSpec((SCAN_ROWS, G_VAL_WIDTH), row),
                  pl.BlockSpec((SCAN_CHUNKS, 4, G_KEY_WIDTH), row3),
                  pl.BlockSpec((SCAN_ROWS, G_VAL_WIDTH), row),
                  pl.BlockSpec((SCAN_ROWS, G_VAL_WIDTH), row),
                  pl.BlockSpec((SCAN_ROWS, D_MODEL), row),
                  pl.BlockSpec((1, G_VAL_WIDTH), const2),
                  pl.BlockSpec((G_VAL_WIDTH, D_MODEL), const2),
                  pl.BlockSpec((1, D_MODEL), const2),
                  pl.BlockSpec((1, D_MODEL), const2)],
        out_specs=pl.BlockSpec((SCAN_ROWS, D_MODEL), row),
        out_shape=jax.ShapeDtypeStruct((m, D_MODEL), F32),
        scratch_shapes=[pltpu.VMEM((G_HEADS, G_KEY_DIM, G_VAL_DIM), F32)],
        compiler_params=pltpu.CompilerParams(dimension_semantics=("arbitrary", "arbitrary"),
                                             vmem_limit_bytes=VMEM_LIMIT_BYTES),
        name="gla_fwd",
    )(p, v, ed, o_b, gate, x2d, head_norm, wo_bf16, ln_g, ln_b)


def _trunk(x, l0, l1):
    n_seq, seq, _ = x.shape
    assert seq % L0_ROWS == 0 and seq % SCAN_ROWS == 0 and SCAN_CHUNKS % 2 == 0
    x2d = x.reshape(n_seq * seq, D_MODEL)
    x1, lr = _l0_layer(x2d, l0["sink"], l0["w_in"], l0["bias"], l0["w_out"], l0["ln_g"], l0["ln_b"],
                       l1["w_lr"], seq // L0_ROWS)
    p_f, v, gate1, ed, o_b = _l1_bwd(x1, lr, l1["w_in"], l1["w_gate"], l1["b_gate"], l1["tril"], l1["triu"],
                                     seq // SCAN_ROWS)
    y = _gla_fwd(p_f, v, ed, o_b, gate1, x1, l1["head_norm"], l1["w_out"], l1["ln_g"], l1["ln_b"],
                 n_seq, seq // SCAN_ROWS)
    return y.reshape(n_seq, seq, D_MODEL)


def kernel(x_prompt, x_sample, l0_w_in, l0_sink, l0_w_out, l0_ln_g, l0_ln_b,
           l1_w_in, l1_w_gate_f, l1_b_gate_f, l1_w_gate_b, l1_b_gate_b, l1_head_norm,
           l1_w_out, l1_ln_g, l1_ln_b):
    row = lambda t: t.reshape(1, -1).astype(F32)
    l0 = dict(w_in=l0_w_in.astype(BF16), sink=l0_sink.astype(F32), bias=_attn_bias_table(),
              w_out=l0_w_out.astype(BF16), ln_g=row(l0_ln_g), ln_b=row(l0_ln_b))
    n_main = 2 * G_KEY_WIDTH + 2 * G_VAL_WIDTH
    w_lr = jnp.zeros((D_MODEL, 128), BF16).at[:, :2 * G_RANK].set(l1_w_in[:, n_main:].astype(BF16))
    w_gate = jnp.zeros((128, 2 * G_KEY_WIDTH), BF16)
    w_gate = w_gate.at[:G_RANK, :G_KEY_WIDTH].set(l1_w_gate_f.astype(BF16))
    w_gate = w_gate.at[G_RANK:2 * G_RANK, G_KEY_WIDTH:].set(l1_w_gate_b.astype(BF16))
    b_gate = jnp.concatenate([l1_b_gate_f, l1_b_gate_b]).reshape(1, -1).astype(F32)
    tri = np.tril(np.ones((G_CHUNK, G_CHUNK), np.float32))
    l1 = dict(w_in=l1_w_in[:, :n_main].astype(BF16), w_lr=w_lr, w_gate=w_gate, b_gate=b_gate,
              tril=jnp.asarray(np.concatenate([tri, tri], axis=1), BF16),
              triu=jnp.asarray(np.concatenate([tri.T, tri.T], axis=1), BF16),
              head_norm=row(l1_head_norm), w_out=l1_w_out.astype(BF16),
              ln_g=row(l1_ln_g), ln_b=row(l1_ln_b))
    return (_trunk(x_prompt, l0, l1), _trunk(x_sample, l0, l1))
```

```python
import functools

import jax
import jax.numpy as jnp
import numpy as np
from jax import lax
from jax.experimental import pallas as pl
from jax.experimental.pallas import tpu as pltpu

F32 = jnp.float32
BF16 = jnp.bfloat16

D_MODEL = 1024
DEPTH = 2
LN_EPS = 1e-5
RMS_EPS = 1e-6
DN_ALPHA = (2 * DEPTH) ** 0.25
NEG = -1e30
LOG2E = 1.4426950408889634

A_HEADS = 16
A_KV_HEADS = 4
A_HEAD_DIM = 64
A_REP = A_HEADS // A_KV_HEADS
A_KV_WIDTH = A_KV_HEADS * A_HEAD_DIM
A_BLOCK = 128
A_PAIRS = A_HEADS // 2
A_WIN = 3 * A_BLOCK
A_QK_AHEAD = 3
A_PV_LAG = 3

G_HEADS = 4
G_KEY_DIM = 128
G_VAL_DIM = 256
G_KEY_WIDTH = G_HEADS * G_KEY_DIM
G_VAL_WIDTH = G_HEADS * G_VAL_DIM
G_RANK = 16
G_TAU = 16.0
G_CHUNK = 128
G_MID = G_CHUNK // 2

L0_SUB = 2
L0_ROWS = L0_SUB * A_BLOCK
SCAN_CHUNKS = 4
SCAN_ROWS = SCAN_CHUNKS * G_CHUNK
V7X_VMEM_BYTES = 64 * 1024 * 1024
VMEM_LIMIT_BYTES = V7X_VMEM_BYTES * 7 // 8


def _layer_norm(y, g, b):
    mu = jnp.mean(y, axis=-1, keepdims=True)
    yc = y - mu
    var = jnp.mean(yc * yc, axis=-1, keepdims=True)
    return yc * lax.rsqrt(var + LN_EPS) * g + b


def _silu(g):
    h = 0.5 * g
    return h + h * jnp.tanh(h)


def _dot(a, b):
    return jnp.dot(a, b, preferred_element_type=F32)


def _dot_nt(a, b):
    return lax.dot_general(a, b, (((1,), (1,)), ((), ())), preferred_element_type=F32)


def _dot_tn(a, b):
    return lax.dot_general(a, b, (((0,), (0,)), ((), ())), preferred_element_type=F32)


def _l0_kernel(sink_ref, x_ref, xo_ref, w_ref, bias_a_ref, bias_b_ref, wo_ref, lng_ref, lnb_ref, wlr_ref,
               o_ref, lr_ref, q_s, g_s, kv_s, a_s):
    t = pl.program_id(0)

    @pl.when(t == 0)
    def _():
        q_s[...] = jnp.zeros_like(q_s)
        g_s[...] = jnp.zeros_like(g_s)
        kv_s[...] = jnp.zeros_like(kv_s)
        a_s[...] = jnp.zeros_like(a_s)

    new2, old2 = lax.rem(t, 2), lax.rem(t + 1, 2)
    k_new, k_cur, k_old = lax.rem(t, 3), lax.rem(t + 2, 3), lax.rem(t + 1, 3)
    q_blk = q_s[old2]
    g_blk = g_s[old2]
    kv_cur = kv_s[k_cur]
    kv_old_tail = kv_s[k_old, A_BLOCK:, :]
    a_prev = a_s[...]
    xb = x_ref[...].astype(BF16)

    zeros = jnp.zeros((A_WIN, A_HEAD_DIM), BF16)
    lane = lax.broadcasted_iota(jnp.int32, (A_BLOCK, 2 * A_HEAD_DIM), 1)
    first_half = lane < A_HEAD_DIM
    first_row_half = lax.broadcasted_iota(jnp.int32, (1, 2 * A_HEAD_DIM), 1) < A_HEAD_DIM

    ones_row = lax.broadcasted_iota(jnp.int32, (2 * A_WIN, 2 * A_HEAD_DIM), 0) < A_WIN
    ones_lane = lax.broadcasted_iota(jnp.int32, (2 * A_WIN, 2 * A_HEAD_DIM), 1) < A_HEAD_DIM
    ones_blk = (ones_row == ones_lane).astype(BF16)

    def slabs(prev, cur, nxt):
        win = jnp.concatenate([prev, cur, nxt], axis=0)
        kk, vv = [], []
        for g in range(A_KV_HEADS):
            kg = win[:, g * A_HEAD_DIM:(g + 1) * A_HEAD_DIM]
            vg = win[:, A_KV_WIDTH + g * A_HEAD_DIM:A_KV_WIDTH + (g + 1) * A_HEAD_DIM]
            kk.append(jnp.concatenate([jnp.concatenate([kg, zeros], axis=1),
                                       jnp.concatenate([zeros, kg], axis=1)], axis=0))
            vv.append(jnp.concatenate([jnp.concatenate([jnp.concatenate([vg, zeros], axis=1),
                                                        jnp.concatenate([zeros, vg], axis=1)], axis=0),
                                       ones_blk], axis=1))
        return kk, vv

    bias_refs = (bias_a_ref, bias_b_ref)
    win_slabs = [slabs(kv_old_tail, kv_cur[:A_BLOCK], kv_cur[A_BLOCK:]), None]
    n_stage = L0_SUB * A_PAIRS

    def scores(j):
        sub, i = divmod(j, A_PAIRS)
        qp = q_blk[sub * A_BLOCK:(sub + 1) * A_BLOCK, i * 128:(i + 1) * 128]
        return _dot_nt(qp, win_slabs[sub][0][i // (A_REP // 2)])

    def softmax(j, s):
        sub, i = divmod(j, A_PAIRS)
        t_ = s + bias_refs[sub][0, i]
        ps, ms = [], []
        for hh in range(2):
            th = t_[:, hh * A_WIN:(hh + 1) * A_WIN]
            m = jnp.max(th, axis=-1, keepdims=True)
            ps.append(jnp.exp2(th - m).astype(BF16))
            ms.append(m)
        sink = jnp.where(first_row_half, sink_ref[2 * i] * LOG2E, sink_ref[2 * i + 1] * LOG2E)
        return jnp.concatenate(ps, axis=1), jnp.exp2(sink - jnp.where(first_half, ms[0], ms[1]))

    def weighted_values(j, p, sink_term):
        sub, i = divmod(j, A_PAIRS)
        rows = slice(sub * A_BLOCK, (sub + 1) * A_BLOCK)
        pv = _dot(p, win_slabs[sub][1][i // (A_REP // 2)])
        o = pv[:, :128] * (1.0 / (pv[:, 128:] + sink_term))
        a_s[rows, i * 128:(i + 1) * 128] = (o * _silu(g_blk[rows, i * 128:(i + 1) * 128].astype(F32))).astype(BF16)

    y_chunks = []
    fw = D_MODEL // 4

    def out_proj_chunk(c):
        y_chunks.append(_dot(a_prev, wo_ref[:, c * fw:(c + 1) * fw]))
        if c == 3:
            y = DN_ALPHA * xo_ref[...] + jnp.concatenate(y_chunks, axis=1)
            y_chunks[:] = [_layer_norm(y, lng_ref[...], lnb_ref[...])]
            o_ref[...] = y_chunks[0]

    def low_rank_chunk(c):
        lr_ref[...] = _dot(y_chunks[0].astype(BF16), wlr_ref[...]).astype(BF16)

    def q_chunk(c):
        acc = _dot(xb, w_ref[:, c * fw:(c + 1) * fw])
        q_s[new2, :, c * fw:(c + 1) * fw] = (acc * (A_HEAD_DIM ** -0.5 * LOG2E)).astype(BF16)

    def gate_chunk(c):
        c0 = D_MODEL + 2 * A_KV_WIDTH + c * fw
        g_s[new2, :, c * fw:(c + 1) * fw] = _dot(xb, w_ref[:, c0:c0 + fw]).astype(BF16)

    kv_parts = []

    def kv_chunk(c):
        c0 = D_MODEL + c * A_KV_WIDTH
        kv_parts.append(_dot(xb, w_ref[:, c0:c0 + A_KV_WIDTH]).astype(BF16))
        if c == 1:
            kv_new = jnp.concatenate(kv_parts, axis=1)
            kv_s[k_new] = kv_new
            win_slabs[1] = slabs(kv_cur[:A_BLOCK], kv_cur[A_BLOCK:], kv_new[:A_BLOCK])

    fillers = [(kv_chunk, c) for c in range(2)] + [(out_proj_chunk, c) for c in range(4)]
    assert len(fillers) <= n_stage and A_QK_AHEAD + 2 <= A_PAIRS

    fillers += ([(q_chunk, c) for c in range(3)] + [(low_rank_chunk, 0), (q_chunk, 3)]
                + [(gate_chunk, c) for c in range(4)])
    assert len(fillers) <= n_stage

    s_ready = {j: scores(j) for j in range(A_QK_AHEAD)}
    pending = []
    for j in range(n_stage):
        if j + A_QK_AHEAD < n_stage:
            s_ready[j + A_QK_AHEAD] = scores(j + A_QK_AHEAD)
        pending.append((j,) + softmax(j, s_ready.pop(j)))
        if len(pending) > A_PV_LAG:
            weighted_values(*pending.pop(0))
        if j < len(fillers):
            fn, c = fillers[j]
            fn(c)
    for item in pending:
        weighted_values(*item)


def _l0_layer(x2d, sink, w_bf16, bias, wo_bf16, ln_g, ln_b, w_lr, blocks_per_seq):
    m = x2d.shape[0]
    done = lambda t: (jnp.maximum(t - 2, 0), 0)
    n_blk = m // L0_ROWS
    n_w = w_bf16.shape[1]
    const2 = lambda t: (0, 0)
    var_a = lambda t: (jnp.where(lax.rem(t + blocks_per_seq - 1, blocks_per_seq) == 0, 0, 1), 0, 0, 0)
    var_b = lambda t: (jnp.where(lax.rem(t, blocks_per_seq) == 0, 2, 1), 0, 0, 0)
    return pl.pallas_call(
        _l0_kernel,
        grid=(n_blk + 2,),
        in_specs=[pl.BlockSpec(memory_space=pltpu.SMEM),
                  pl.BlockSpec((L0_ROWS, D_MODEL), lambda t: (jnp.minimum(t, n_blk - 1), 0)),
                  pl.BlockSpec((L0_ROWS, D_MODEL), done),
                  pl.BlockSpec((D_MODEL, n_w), const2),
                  pl.BlockSpec((1, A_PAIRS, A_BLOCK, 2 * A_WIN), var_a),
                  pl.BlockSpec((1, A_PAIRS, A_BLOCK, 2 * A_WIN), var_b),
                  pl.BlockSpec((D_MODEL, D_MODEL), const2),
                  pl.BlockSpec((1, D_MODEL), const2),
                  pl.BlockSpec((1, D_MODEL), const2),
                  pl.BlockSpec((D_MODEL, 128), const2)],
        out_specs=[pl.BlockSpec((L0_ROWS, D_MODEL), done),
                   pl.BlockSpec((L0_ROWS, 128), done)],
        out_shape=[jax.ShapeDtypeStruct((m, D_MODEL), F32),
                   jax.ShapeDtypeStruct((m, 128), BF16)],
        scratch_shapes=[pltpu.VMEM((2, L0_ROWS, D_MODEL), BF16),
                        pltpu.VMEM((2, L0_ROWS, D_MODEL), BF16),
                        pltpu.VMEM((3, L0_ROWS, 2 * A_KV_WIDTH), BF16),
                        pltpu.VMEM((L0_ROWS, D_MODEL), BF16)],
        compiler_params=pltpu.CompilerParams(dimension_semantics=("arbitrary",),
                                             vmem_limit_bytes=VMEM_LIMIT_BYTES),
        name="l0_layer",
    )(sink, x2d, x2d, w_bf16, bias, bias, wo_bf16, ln_g, ln_b, w_lr)


def _attn_bias_table():
    qi = np.arange(A_BLOCK)[:, None] + A_BLOCK
    kj = np.arange(A_WIN)[None, :]
    dist = np.abs(qi - kj).astype(np.float32)
    in_win = dist <= A_BLOCK
    slopes = (2.0 ** (-8.0 * np.arange(1, A_HEADS + 1) / A_HEADS)).astype(np.float32)
    per_head = (-slopes[:, None, None] * dist[None]).astype(np.float32) * np.float32(LOG2E)
    variants = []
    for valid in (kj >= A_BLOCK, kj >= 0, kj < 2 * A_BLOCK):
        b = np.where((in_win & valid)[None], per_head, np.float32(NEG)).astype(np.float32)
        variants.append(b.reshape(A_PAIRS, 2, A_BLOCK, A_WIN).transpose(0, 2, 1, 3).reshape(A_PAIRS, A_BLOCK, 2 * A_WIN))
    return jnp.asarray(np.stack(variants))


def _log2_decay(z):
    return (jnp.minimum(z, 0.0) - jnp.log(1.0 + jnp.exp(-jnp.abs(z)))) * (LOG2E / G_TAU)


def _split_cumsum(tri2_bf16, la):
    hi = la.astype(BF16)
    lo = (la - hi.astype(F32)).astype(BF16)
    return _dot(tri2_bf16, jnp.concatenate([hi, lo], axis=0))


class _GlaScan:
    def __init__(self, p_ref, v_ref, ed_ref, state_ref, backward):
        rows = lax.broadcasted_iota(jnp.int32, (G_CHUNK, G_CHUNK), 0)
        cols = lax.broadcasted_iota(jnp.int32, (G_CHUNK, G_CHUNK), 1)
        self.eye = (rows == cols).astype(F32)
        self.mask = rows < cols if backward else rows >= cols
        self.d = 1 if backward else 0
        self.order = list(range(SCAN_CHUNKS))[::-1] if backward else list(range(SCAN_CHUNKS))
        self.p_ref, self.v_ref, self.ed_ref, self.state_ref = p_ref, v_ref, ed_ref, state_ref
        self.states = [state_ref[h] for h in range(G_HEADS)]

    @staticmethod
    def _rows(c):
        return slice(c * G_CHUNK, (c + 1) * G_CHUNK)

    @staticmethod
    def _kcols(part, h):
        return slice(part * G_KEY_WIDTH + h * G_KEY_DIM, part * G_KEY_WIDTH + (h + 1) * G_KEY_DIM)

    @staticmethod
    def _vcols(h):
        return slice(h * G_VAL_DIM, (h + 1) * G_VAL_DIM)

    def attention(self):
        p = self.p_ref
        self.atts = {(c, h): jnp.where(self.mask,
                                       _dot_nt(p[self._rows(c), self._kcols(0, h)], p[self._rows(c), self._kcols(1, h)]),
                                       0.0).astype(BF16)
                     for c in self.order for h in range(G_HEADS)}

    def chunk(self, c):
        p, v, ed, d = self.p_ref, self.v_ref, self.ed_ref, self.d
        inters = []
        for h in range(G_HEADS):
            e_row = ed[c, 2 * d:2 * d + 1, h * G_KEY_DIM:(h + 1) * G_KEY_DIM]
            d_row = ed[c, 2 * d + 1:2 * d + 2, h * G_KEY_DIM:(h + 1) * G_KEY_DIM]
            e_col = jnp.sum(self.eye * e_row, axis=1, keepdims=True)
            d_col = jnp.sum(self.eye * d_row, axis=1, keepdims=True)
            inters.append(_dot(p[self._rows(c), self._kcols(0, h)], (self.states[h] * e_col).astype(BF16)))
            self.states[h] = (self.states[h] * d_col
                              + _dot_tn(p[self._rows(c), self._kcols(2, h)], v[self._rows(c), self._vcols(h)]))
        return [_dot(self.atts[c, h], v[self._rows(c), self._vcols(h)]) + inters[h] for h in range(G_HEADS)]

    def finish(self):
        for h in range(G_HEADS):
            self.state_ref[h] = self.states[h]


def _l1_bwd_kernel(x_ref, lr_ref, w_ref, wg_ref, bg_ref, tril_ref, triu_ref,
                   pf_ref, v_ref, g_ref, ed_ref, ob_ref,
                   pb_s, v_s, ed_s, state_s, *, n_blk, blocks_per_seq):
    t = pl.program_id(0)

    @pl.when(t == 0)
    def _():
        pb_s[...] = jnp.zeros_like(pb_s)
        v_s[...] = jnp.zeros_like(v_s)
        ed_s[...] = jnp.zeros_like(ed_s)

    @pl.when((t == 0) | (lax.rem(n_blk - t + blocks_per_seq, blocks_per_seq) == blocks_per_seq - 1))
    def _():
        state_s[...] = jnp.zeros_like(state_s)

    kw = G_KEY_WIDTH
    scan = _GlaScan(pb_s, v_s, ed_s, state_s, backward=True)
    scan.attention()
    pending_chunks = list(scan.order)
    outs = {}

    def scan_piece():
        if pending_chunks:
            c = pending_chunks.pop(0)
            outs[c] = scan.chunk(c)

    half_rows = SCAN_ROWS // 2
    half_chunks = SCAN_CHUNKS // 2
    pb_new, ed_new, v_new = {}, {}, []
    for hf in range(2):
        rs = slice(hf * half_rows, (hf + 1) * half_rows)
        xb = x_ref[rs, :].astype(BF16)
        z = _dot(lr_ref[rs, :], wg_ref[...]) + bg_ref[...]
        q = _dot(xb, w_ref[:, 0:kw]) * (G_KEY_DIM ** -0.5)
        k = _dot(xb, w_ref[:, kw:2 * kw])
        scan_piece()
        la = _log2_decay(z)
        cums = [(_split_cumsum(tril_ref[...], la[c * G_CHUNK:(c + 1) * G_CHUNK, 0:kw]),
                 _split_cumsum(triu_ref[...], la[c * G_CHUNK:(c + 1) * G_CHUNK, kw:2 * kw]))
                for c in range(half_chunks)]
        v_half = jnp.concatenate([_dot(xb, w_ref[:, 2 * kw + c * kw:2 * kw + (c + 1) * kw]).astype(BF16)
                                  for c in range(2)], axis=1)
        v_ref[rs, :] = v_half
        v_new.append(v_half)
        for cl in range(half_chunks):
            c = hf * half_chunks + cl
            r0 = c * G_CHUNK
            qc = q[cl * G_CHUNK:(cl + 1) * G_CHUNK]
            kc = k[cl * G_CHUNK:(cl + 1) * G_CHUNK]
            b, s = cums[cl]
            b_mid = b[G_MID - 1:G_MID]
            b_end = b[G_CHUNK - 1:G_CHUNK]
            pf_ref[r0:r0 + G_CHUNK, 0:kw] = (qc * jnp.exp2(b - b_mid)).astype(BF16)
            pf_ref[r0:r0 + G_CHUNK, kw:2 * kw] = (kc * jnp.exp2(b_mid - b)).astype(BF16)
            pf_ref[r0:r0 + G_CHUNK, 2 * kw:3 * kw] = (kc * jnp.exp2(b_end - b)).astype(BF16)
            s_mid = s[G_MID:G_MID + 1]
            s_end = s[0:1]
            pb_new[c] = jnp.concatenate([(qc * jnp.exp2(s - s_mid)).astype(BF16),
                                         (kc * jnp.exp2(s_mid - s)).astype(BF16),
                                         (kc * jnp.exp2(s_end - s)).astype(BF16)], axis=1)
            ed_new[c] = jnp.concatenate([jnp.exp2(b_mid), jnp.exp2(b_end), jnp.exp2(s_mid), jnp.exp2(s_end)], axis=0)
            ed_ref[c] = ed_new[c]
        for c in range(2):
            c0 = 2 * kw + G_VAL_WIDTH + c * kw
            g_ref[rs, c * kw:(c + 1) * kw] = _silu(_dot(xb, w_ref[:, c0:c0 + kw])).astype(BF16)
        scan_piece()
    while pending_chunks:
        scan_piece()
    scan.finish()
    for c, per_head in outs.items():
        for h, o in enumerate(per_head):
            ob_ref[c * G_CHUNK:(c + 1) * G_CHUNK, h * G_VAL_DIM:(h + 1) * G_VAL_DIM] = o.astype(BF16)
    for c in range(SCAN_CHUNKS):
        pb_s[c * G_CHUNK:(c + 1) * G_CHUNK, :] = pb_new[c]
        ed_s[c] = ed_new[c]
    v_s[...] = jnp.concatenate(v_new, axis=0)


def _l1_bwd(x2d, lr, w_bf16, wg, bg, tril, triu, blocks_per_seq):
    m = x2d.shape[0]
    n_blk = m // SCAN_ROWS
    n_main = w_bf16.shape[1]
    const2 = lambda t: (0, 0)
    proj = lambda t: (jnp.maximum(n_blk - 1 - t, 0), 0)
    proj3 = lambda t: (jnp.maximum(n_blk - 1 - t, 0), 0, 0)
    scanned = lambda t: (jnp.minimum(n_blk - t, n_blk - 1), 0)
    return pl.pallas_call(
        functools.partial(_l1_bwd_kernel, n_blk=n_blk, blocks_per_seq=blocks_per_seq),
        grid=(n_blk + 1,),
        in_specs=[pl.BlockSpec((SCAN_ROWS, D_MODEL), proj),
                  pl.BlockSpec((SCAN_ROWS, 128), proj),
                  pl.BlockSpec((D_MODEL, n_main), const2),
                  pl.BlockSpec((128, 2 * G_KEY_WIDTH), const2),
                  pl.BlockSpec((1, 2 * G_KEY_WIDTH), const2),
                  pl.BlockSpec((G_CHUNK, 2 * G_CHUNK), const2),
                  pl.BlockSpec((G_CHUNK, 2 * G_CHUNK), const2)],
        out_specs=[pl.BlockSpec((SCAN_ROWS, 3 * G_KEY_WIDTH), proj),
                   pl.BlockSpec((SCAN_ROWS, G_VAL_WIDTH), proj),
                   pl.BlockSpec((SCAN_ROWS, G_VAL_WIDTH), proj),
                   pl.BlockSpec((SCAN_CHUNKS, 4, G_KEY_WIDTH), proj3),
                   pl.BlockSpec((SCAN_ROWS, G_VAL_WIDTH), scanned)],
        out_shape=[jax.ShapeDtypeStruct((m, 3 * G_KEY_WIDTH), BF16),
                   jax.ShapeDtypeStruct((m, G_VAL_WIDTH), BF16),
                   jax.ShapeDtypeStruct((m, G_VAL_WIDTH), BF16),
                   jax.ShapeDtypeStruct((m // G_CHUNK, 4, G_KEY_WIDTH), F32),
                   jax.ShapeDtypeStruct((m, G_VAL_WIDTH), BF16)],
        scratch_shapes=[pltpu.VMEM((SCAN_ROWS, 3 * G_KEY_WIDTH), BF16),
                        pltpu.VMEM((SCAN_ROWS, G_VAL_WIDTH), BF16),
                        pltpu.VMEM((SCAN_CHUNKS, 4, G_KEY_WIDTH), F32),
                        pltpu.VMEM((G_HEADS, G_KEY_DIM, G_VAL_DIM), F32)],
        compiler_params=pltpu.CompilerParams(dimension_semantics=("arbitrary",),
                                             vmem_limit_bytes=VMEM_LIMIT_BYTES),
        name="l1_bwd",
    )(x2d, lr, w_bf16, wg, bg, tril, triu)


def _gla_fwd_kernel(p_ref, v_ref, ed_ref, ob_ref, g_ref, x_ref, hn_ref, wo_ref, lng_ref, lnb_ref,
                    o_ref, state_ref):
    @pl.when(pl.program_id(1) == 0)
    def _():
        state_ref[...] = jnp.zeros_like(state_ref)

    scan = _GlaScan(p_ref, v_ref, ed_ref, state_ref, backward=False)
    scan.attention()
    outs = {c: scan.chunk(c) for c in scan.order}
    scan.finish()
    acts = []
    for c in range(SCAN_CHUNKS):
        rsl = slice(c * G_CHUNK, (c + 1) * G_CHUNK)
        segs = []
        for h in range(G_HEADS):
            seg = outs[c][h] + ob_ref[rsl, h * G_VAL_DIM:(h + 1) * G_VAL_DIM].astype(F32)
            ms = jnp.mean(seg * seg, axis=-1, keepdims=True)
            segs.append(seg * lax.rsqrt(ms + RMS_EPS))
        on = jnp.concatenate(segs, axis=1) * hn_ref[...]
        acts.append((on * g_ref[rsl, :].astype(F32)).astype(BF16))
    half = SCAN_CHUNKS // 2
    for hf in range(2):
        rs = slice(hf * half * G_CHUNK, (hf + 1) * half * G_CHUNK)
        y = DN_ALPHA * x_ref[rs, :] + _dot(jnp.concatenate(acts[hf * half:(hf + 1) * half], axis=0), wo_ref[...])
        o_ref[rs, :] = _layer_norm(y, lng_ref[...], lnb_ref[...])


def _gla_fwd(p, v, ed, o_b, gate, x2d, head_norm, wo_bf16, ln_g, ln_b, n_seq, n_step):
    m = x2d.shape[0]
    row = lambda b, j: (b * n_step + j, 0)
    row3 = lambda b, j: (b * n_step + j, 0, 0)
    const2 = lambda b, j: (0, 0)
    return pl.pallas_call(
        _gla_fwd_kernel,
        grid=(n_seq, n_step),
        in_specs=[pl.BlockSpec((SCAN_ROWS, 3 * G_KEY_WIDTH), row),
                  pl.BlockSpec((SCAN_ROWS, G_VAL_WIDTH), row),
                  pl.BlockSpec((SCAN_CHUNKS, 4, G_KEY_WIDTH), row3),
                  pl.BlockSpec((SCAN_ROWS, G_VAL_WIDTH), row),
                  pl.BlockSpec((SCAN_ROWS, G_VAL_WIDTH), row),
                  pl.BlockSpec((SCAN_ROWS, D_MODEL), row),
                  pl.BlockSpec((1, G_VAL_WIDTH), const2),
                  pl.BlockSpec((G_VAL_WIDTH, D_MODEL), const2),
                  pl.BlockSpec((1, D_MODEL), const2),
                  pl.BlockSpec((1, D_MODEL), const2)],
        out_specs=pl.BlockSpec((SCAN_ROWS, D_MODEL), row),
        out_shape=jax.ShapeDtypeStruct((m, D_MODEL), F32),
        scratch_shapes=[pltpu.VMEM((G_HEADS, G_KEY_DIM, G_VAL_DIM), F32)],
        compiler_params=pltpu.CompilerParams(dimension_semantics=("arbitrary", "arbitrary"),
                                             vmem_limit_bytes=VMEM_LIMIT_BYTES),
        name="gla_fwd",
    )(p, v, ed, o_b, gate, x2d, head_norm, wo_bf16, ln_g, ln_b)


def _trunk(x, l0, l1):
    n_seq, seq, _ = x.shape
    assert seq % L0_ROWS == 0 and seq % SCAN_ROWS == 0 and SCAN_CHUNKS % 2 == 0
    x2d = x.reshape(n_seq * seq, D_MODEL)
    x1, lr = _l0_layer(x2d, l0["sink"], l0["w_in"], l0["bias"], l0["w_out"], l0["ln_g"], l0["ln_b"],
                       l1["w_lr"], seq // L0_ROWS)
    p_f, v, gate1, ed, o_b = _l1_bwd(x1, lr, l1["w_in"], l1["w_gate"], l1["b_gate"], l1["tril"], l1["triu"],
                                     seq // SCAN_ROWS)
    y = _gla_fwd(p_f, v, ed, o_b, gate1, x1, l1["head_norm"], l1["w_out"], l1["ln_g"], l1["ln_b"],
                 n_seq, seq // SCAN_ROWS)
    return y.reshape(n_seq, seq, D_MODEL)


def kernel(x_prompt, x_sample, l0_w_in, l0_sink, l0_w_out, l0_ln_g, l0_ln_b,
           l1_w_in, l1_w_gate_f, l1_b_gate_f, l1_w_gate_b, l1_b_gate_b, l1_head_norm,
           l1_w_out, l1_ln_g, l1_ln_b):
    row = lambda t: t.reshape(1, -1).astype(F32)
    l0 = dict(w_in=l0_w_in.astype(BF16), sink=l0_sink.astype(F32), bias=_attn_bias_table(),
              w_out=l0_w_out.astype(BF16), ln_g=row(l0_ln_g), ln_b=row(l0_ln_b))
    n_main = 2 * G_KEY_WIDTH + 2 * G_VAL_WIDTH
    w_lr = jnp.zeros((D_MODEL, 128), BF16).at[:, :2 * G_RANK].set(l1_w_in[:, n_main:].astype(BF16))
    w_gate = jnp.zeros((128, 2 * G_KEY_WIDTH), BF16)
    w_gate = w_gate.at[:G_RANK, :G_KEY_WIDTH].set(l1_w_gate_f.astype(BF16))
    w_gate = w_gate.at[G_RANK:2 * G_RANK, G_KEY_WIDTH:].set(l1_w_gate_b.astype(BF16))
    b_gate = jnp.concatenate([l1_b_gate_f, l1_b_gate_b]).reshape(1, -1).astype(F32)
    tri = np.tril(np.ones((G_CHUNK, G_CHUNK), np.float32))
    l1 = dict(w_in=l1_w_in[:, :n_main].astype(BF16), w_lr=w_lr, w_gate=w_gate, b_gate=b_gate,
              tril=jnp.asarray(np.concatenate([tri, tri], axis=1), BF16),
              triu=jnp.asarray(np.concatenate([tri.T, tri.T], axis=1), BF16),
              head_norm=row(l1_head_norm), w_out=l1_w_out.astype(BF16),
              ln_g=row(l1_ln_g), ln_b=row(l1_ln_b))
    return (_trunk(x_prompt, l0, l1), _trunk(x_sample, l0, l1))
```

```python
import functools

import jax
import jax.numpy as jnp
import numpy as np
from jax import lax
from jax.experimental import pallas as pl
from jax.experimental.pallas import tpu as pltpu

F32 = jnp.float32
BF16 = jnp.bfloat16

D_MODEL = 1024
DEPTH = 2
LN_EPS = 1e-5
RMS_EPS = 1e-6
DN_ALPHA = (2 * DEPTH) ** 0.25
NEG = -1e30
LOG2E = 1.4426950408889634

A_HEADS = 16
A_KV_HEADS = 4
A_HEAD_DIM = 64
A_REP = A_HEADS // A_KV_HEADS
A_KV_WIDTH = A_KV_HEADS * A_HEAD_DIM
A_BLOCK = 128
A_PAIRS = A_HEADS // 2
A_WIN = 3 * A_BLOCK
A_QK_AHEAD = 3
A_PV_LAG = 3

G_HEADS = 4
G_KEY_DIM = 128
G_VAL_DIM = 256
G_KEY_WIDTH = G_HEADS * G_KEY_DIM
G_VAL_WIDTH = G_HEADS * G_VAL_DIM
G_RANK = 16
G_TAU = 16.0
G_CHUNK = 128
G_MID = G_CHUNK // 2

L0_SUB = 2
L0_ROWS = L0_SUB * A_BLOCK
SCAN_CHUNKS = 4
SCAN_ROWS = SCAN_CHUNKS * G_CHUNK
V7X_VMEM_BYTES = 64 * 1024 * 1024
VMEM_LIMIT_BYTES = V7X_VMEM_BYTES * 7 // 8


def _layer_norm(y, g, b):
    mu = jnp.mean(y, axis=-1, keepdims=True)
    yc = y - mu
    var = jnp.mean(yc * yc, axis=-1, keepdims=True)
    return yc * lax.rsqrt(var + LN_EPS) * g + b


def _silu(g):
    h = 0.5 * g
    return h + h * jnp.tanh(h)


def _dot(a, b):
    return jnp.dot(a, b, preferred_element_type=F32)


def _dot_nt(a, b):
    return lax.dot_general(a, b, (((1,), (1,)), ((), ())), preferred_element_type=F32)


def _dot_tn(a, b):
    return lax.dot_general(a, b, (((0,), (0,)), ((), ())), preferred_element_type=F32)


def _l0_kernel(sink_ref, x_ref, xo_ref, w_ref, bias_a_ref, bias_b_ref, wo_ref, lng_ref, lnb_ref, wlr_ref,
               o_ref, lr_ref, q_s, g_s, kv_s, a_s):
    t = pl.program_id(0)

    @pl.when(t == 0)
    def _():
        q_s[...] = jnp.zeros_like(q_s)
        g_s[...] = jnp.zeros_like(g_s)
        kv_s[...] = jnp.zeros_like(kv_s)
        a_s[...] = jnp.zeros_like(a_s)

    new2, old2 = lax.rem(t, 2), lax.rem(t + 1, 2)
    k_new, k_cur, k_old = lax.rem(t, 3), lax.rem(t + 2, 3), lax.rem(t + 1, 3)
    q_blk = q_s[old2]
    g_blk = g_s[old2]
    kv_cur = kv_s[k_cur]
    kv_old_tail = kv_s[k_old, A_BLOCK:, :]
    a_prev = a_s[...]
    xb = x_ref[...].astype(BF16)

    zeros = jnp.zeros((A_WIN, A_HEAD_DIM), BF16)
    lane = lax.broadcasted_iota(jnp.int32, (A_BLOCK, 2 * A_HEAD_DIM), 1)
    first_half = lane < A_HEAD_DIM
    first_row_half = lax.broadcasted_iota(jnp.int32, (1, 2 * A_HEAD_DIM), 1) < A_HEAD_DIM

    ones_row = lax.broadcasted_iota(jnp.int32, (2 * A_WIN, 2 * A_HEAD_DIM), 0) < A_WIN
    ones_lane = lax.broadcasted_iota(jnp.int32, (2 * A_WIN, 2 * A_HEAD_DIM), 1) < A_HEAD_DIM
    ones_blk = (ones_row == ones_lane).astype(BF16)

    def slabs(prev, cur, nxt):
        win = jnp.concatenate([prev, cur, nxt], axis=0)
        kk, vv = [], []
        for g in range(A_KV_HEADS):
            kg = win[:, g * A_HEAD_DIM:(g + 1) * A_HEAD_DIM]
            vg = win[:, A_KV_WIDTH + g * A_HEAD_DIM:A_KV_WIDTH + (g + 1) * A_HEAD_DIM]
            kk.append(jnp.concatenate([jnp.concatenate([kg, zeros], axis=1),
                                       jnp.concatenate([zeros, kg], axis=1)], axis=0))
            vv.append(jnp.concatenate([jnp.concatenate([jnp.concatenate([vg, zeros], axis=1),
                                                        jnp.concatenate([zeros, vg], axis=1)], axis=0),
                                       ones_blk], axis=1))
        return kk, vv

    bias_refs = (bias_a_ref, bias_b_ref)
    win_slabs = [slabs(kv_old_tail, kv_cur[:A_BLOCK], kv_cur[A_BLOCK:]), None]
    n_stage = L0_SUB * A_PAIRS

    def scores(j):
        sub, i = divmod(j, A_PAIRS)
        qp = q_blk[sub * A_BLOCK:(sub + 1) * A_BLOCK, i * 128:(i + 1) * 128]
        return _dot_nt(qp, win_slabs[sub][0][i // (A_REP // 2)])

    def softmax(j, s):
        sub, i = divmod(j, A_PAIRS)
        t_ = s + bias_refs[sub][0, i]
        ps, ms = [], []
        for hh in range(2):
            th = t_[:, hh * A_WIN:(hh + 1) * A_WIN]
            m = jnp.max(th, axis=-1, keepdims=True)
            ps.append(jnp.exp2(th - m).astype(BF16))
            ms.append(m)
        sink = jnp.where(first_row_half, sink_ref[2 * i] * LOG2E, sink_ref[2 * i + 1] * LOG2E)
        return jnp.concatenate(ps, axis=1), jnp.exp2(sink - jnp.where(first_half, ms[0], ms[1]))

    def weighted_values(j, p, sink_term):
        sub, i = divmod(j, A_PAIRS)
        rows = slice(sub * A_BLOCK, (sub + 1) * A_BLOCK)
        pv = _dot(p, win_slabs[sub][1][i // (A_REP // 2)])
        o = pv[:, :128] * (1.0 / (pv[:, 128:] + sink_term))
        a_s[rows, i * 128:(i + 1) * 128] = (o * _silu(g_blk[rows, i * 128:(i + 1) * 128].astype(F32))).astype(BF16)

    y_chunks = []
    fw = D_MODEL // 4

    def out_proj_chunk(c):
        y_chunks.append(_dot(a_prev, wo_ref[:, c * fw:(c + 1) * fw]))
        if c == 3:
            y = DN_ALPHA * xo_ref[...] + jnp.concatenate(y_chunks, axis=1)
            y_chunks[:] = [_layer_norm(y, lng_ref[...], lnb_ref[...])]
            o_ref[...] = y_chunks[0]

    def low_rank_chunk(c):
        lr_ref[...] = _dot(y_chunks[0].astype(BF16), wlr_ref[...]).astype(BF16)

    def q_chunk(c):
        acc = _dot(xb, w_ref[:, c * fw:(c + 1) * fw])
        q_s[new2, :, c * fw:(c + 1) * fw] = (acc * (A_HEAD_DIM ** -0.5 * LOG2E)).astype(BF16)

    def gate_chunk(c):
        c0 = D_MODEL + 2 * A_KV_WIDTH + c * fw
        g_s[new2, :, c * fw:(c + 1) * fw] = _dot(xb, w_ref[:, c0:c0 + fw]).astype(BF16)

    kv_parts = []

    def kv_chunk(c):
        c0 = D_MODEL + c * A_KV_WIDTH
        kv_parts.append(_dot(xb, w_ref[:, c0:c0 + A_KV_WIDTH]).astype(BF16))
        if c == 1:
            kv_new = jnp.concatenate(kv_parts, axis=1)
            kv_s[k_new] = kv_new
            win_slabs[1] = slabs(kv_cur[:A_BLOCK], kv_cur[A_BLOCK:], kv_new[:A_BLOCK])

    fillers = [(kv_chunk, c) for c in range(2)] + [(out_proj_chunk, c) for c in range(4)]
    assert len(fillers) <= n_stage and A_QK_AHEAD + 2 <= A_PAIRS

    fillers += ([(q_chunk, c) for c in range(3)] + [(low_rank_chunk, 0), (q_chunk, 3)]
                + [(gate_chunk, c) for c in range(4)])
    assert len(fillers) + 1 <= n_stage

    s_ready = {j: scores(j) for j in range(A_QK_AHEAD)}
    pending = []
    for j in range(n_stage):
        if j + A_QK_AHEAD < n_stage:
            s_ready[j + A_QK_AHEAD] = scores(j + A_QK_AHEAD)
        pending.append((j,) + softmax(j, s_ready.pop(j)))
        if len(pending) > A_PV_LAG:
            weighted_values(*pending.pop(0))
        if 1 <= j <= len(fillers):
            fn, c = fillers[j - 1]
            fn(c)
    for item in pending:
        weighted_values(*item)


def _l0_layer(x2d, sink, w_bf16, bias, wo_bf16, ln_g, ln_b, w_lr, blocks_per_seq):
    m = x2d.shape[0]
    done = lambda t: (jnp.maximum(t - 2, 0), 0)
    n_blk = m // L0_ROWS
    n_w = w_bf16.shape[1]
    const2 = lambda t: (0, 0)
    var_a = lambda t: (jnp.where(lax.rem(t + blocks_per_seq - 1, blocks_per_seq) == 0, 0, 1), 0, 0, 0)
    var_b = lambda t: (jnp.where(lax.rem(t, blocks_per_seq) == 0, 2, 1), 0, 0, 0)
    return pl.pallas_call(
        _l0_kernel,
        grid=(n_blk + 2,),
        in_specs=[pl.BlockSpec(memory_space=pltpu.SMEM),
                  pl.BlockSpec((L0_ROWS, D_MODEL), lambda t: (jnp.minimum(t, n_blk - 1), 0)),
                  pl.BlockSpec((L0_ROWS, D_MODEL), done),
                  pl.BlockSpec((D_MODEL, n_w), const2),
                  pl.BlockSpec((1, A_PAIRS, A_BLOCK, 2 * A_WIN), var_a),
                  pl.BlockSpec((1, A_PAIRS, A_BLOCK, 2 * A_WIN), var_b),
                  pl.BlockSpec((D_MODEL, D_MODEL), const2),
                  pl.BlockSpec((1, D_MODEL), const2),
                  pl.BlockSpec((1, D_MODEL), const2),
                  pl.BlockSpec((D_MODEL, 128), const2)],
        out_specs=[pl.BlockSpec((L0_ROWS, D_MODEL), done),
                   pl.BlockSpec((L0_ROWS, 128), done)],
        out_shape=[jax.ShapeDtypeStruct((m, D_MODEL), F32),
                   jax.ShapeDtypeStruct((m, 128), BF16)],
        scratch_shapes=[pltpu.VMEM((2, L0_ROWS, D_MODEL), BF16),
                        pltpu.VMEM((2, L0_ROWS, D_MODEL), BF16),
                        pltpu.VMEM((3, L0_ROWS, 2 * A_KV_WIDTH), BF16),
                        pltpu.VMEM((L0_ROWS, D_MODEL), BF16)],
        compiler_params=pltpu.CompilerParams(dimension_semantics=("arbitrary",),
                                             vmem_limit_bytes=VMEM_LIMIT_BYTES),
        name="l0_layer",
    )(sink, x2d, x2d, w_bf16, bias, bias, wo_bf16, ln_g, ln_b, w_lr)


def _attn_bias_table():
    qi = np.arange(A_BLOCK)[:, None] + A_BLOCK
    kj = np.arange(A_WIN)[None, :]
    dist = np.abs(qi - kj).astype(np.float32)
    in_win = dist <= A_BLOCK
    slopes = (2.0 ** (-8.0 * np.arange(1, A_HEADS + 1) / A_HEADS)).astype(np.float32)
    per_head = (-slopes[:, None, None] * dist[None]).astype(np.float32) * np.float32(LOG2E)
    variants = []
    for valid in (kj >= A_BLOCK, kj >= 0, kj < 2 * A_BLOCK):
        b = np.where((in_win & valid)[None], per_head, np.float32(NEG)).astype(np.float32)
        variants.append(b.reshape(A_PAIRS, 2, A_BLOCK, A_WIN).transpose(0, 2, 1, 3).reshape(A_PAIRS, A_BLOCK, 2 * A_WIN))
    return jnp.asarray(np.stack(variants))


def _log2_decay(z):
    return (jnp.minimum(z, 0.0) - jnp.log(1.0 + jnp.exp(-jnp.abs(z)))) * (LOG2E / G_TAU)


def _split_cumsum(tri2_bf16, la):
    hi = la.astype(BF16)
    lo = (la - hi.astype(F32)).astype(BF16)
    return _dot(tri2_bf16, jnp.concatenate([hi, lo], axis=0))


class _GlaScan:
    def __init__(self, p_ref, v_ref, ed_ref, state_ref, backward):
        rows = lax.broadcasted_iota(jnp.int32, (G_CHUNK, G_CHUNK), 0)
        cols = lax.broadcasted_iota(jnp.int32, (G_CHUNK, G_CHUNK), 1)
        self.eye = (rows == cols).astype(F32)
        self.mask = rows < cols if backward else rows >= cols
        self.d = 1 if backward else 0
        self.order = list(range(SCAN_CHUNKS))[::-1] if backward else list(range(SCAN_CHUNKS))
        self.p_ref, self.v_ref, self.ed_ref, self.state_ref = p_ref, v_ref, ed_ref, state_ref
        self.states = [state_ref[h] for h in range(G_HEADS)]

    @staticmethod
    def _rows(c):
        return slice(c * G_CHUNK, (c + 1) * G_CHUNK)

    @staticmethod
    def _kcols(part, h):
        return slice(part * G_KEY_WIDTH + h * G_KEY_DIM, part * G_KEY_WIDTH + (h + 1) * G_KEY_DIM)

    @staticmethod
    def _vcols(h):
        return slice(h * G_VAL_DIM, (h + 1) * G_VAL_DIM)

    def attention(self):
        p = self.p_ref
        self.atts = {(c, h): jnp.where(self.mask,
                                       _dot_nt(p[self._rows(c), self._kcols(0, h)], p[self._rows(c), self._kcols(1, h)]),
                                       0.0).astype(BF16)
                     for c in self.order for h in range(G_HEADS)}

    def chunk(self, c):
        p, v, ed, d = self.p_ref, self.v_ref, self.ed_ref, self.d
        inters = []
        for h in range(G_HEADS):
            e_row = ed[c, 2 * d:2 * d + 1, h * G_KEY_DIM:(h + 1) * G_KEY_DIM]
            d_row = ed[c, 2 * d + 1:2 * d + 2, h * G_KEY_DIM:(h + 1) * G_KEY_DIM]
            e_col = jnp.sum(self.eye * e_row, axis=1, keepdims=True)
            d_col = jnp.sum(self.eye * d_row, axis=1, keepdims=True)
            inters.append(_dot(p[self._rows(c), self._kcols(0, h)], (self.states[h] * e_col).astype(BF16)))
            self.states[h] = (self.states[h] * d_col
                              + _dot_tn(p[self._rows(c), self._kcols(2, h)], v[self._rows(c), self._vcols(h)]))
        return [_dot(self.atts[c, h], v[self._rows(c), self._vcols(h)]) + inters[h] for h in range(G_HEADS)]

    def finish(self):
        for h in range(G_HEADS):
            self.state_ref[h] = self.states[h]


def _l1_bwd_kernel(x_ref, lr_ref, w_ref, wg_ref, bg_ref, tril_ref, triu_ref,
                   pf_ref, v_ref, g_ref, ed_ref, ob_ref,
                   pb_s, v_s, ed_s, state_s, *, n_blk, blocks_per_seq):
    t = pl.program_id(0)

    @pl.when(t == 0)
    def _():
        pb_s[...] = jnp.zeros_like(pb_s)
        v_s[...] = jnp.zeros_like(v_s)
        ed_s[...] = jnp.zeros_like(ed_s)

    @pl.when((t == 0) | (lax.rem(n_blk - t + blocks_per_seq, blocks_per_seq) == blocks_per_seq - 1))
    def _():
        state_s[...] = jnp.zeros_like(state_s)

    kw = G_KEY_WIDTH
    scan = _GlaScan(pb_s, v_s, ed_s, state_s, backward=True)
    scan.attention()
    pending_chunks = list(scan.order)
    outs = {}

    def scan_piece():
        if pending_chunks:
            c = pending_chunks.pop(0)
            outs[c] = scan.chunk(c)

    half_rows = SCAN_ROWS // 2
    half_chunks = SCAN_CHUNKS // 2
    pb_new, ed_new, v_new = {}, {}, []
    for hf in range(2):
        rs = slice(hf * half_rows, (hf + 1) * half_rows)
        xb = x_ref[rs, :].astype(BF16)
        z = _dot(lr_ref[rs, :], wg_ref[...]) + bg_ref[...]
        q = _dot(xb, w_ref[:, 0:kw]) * (G_KEY_DIM ** -0.5)
        k = _dot(xb, w_ref[:, kw:2 * kw])
        scan_piece()
        la = _log2_decay(z)
        cums = [(_split_cumsum(tril_ref[...], la[c * G_CHUNK:(c + 1) * G_CHUNK, 0:kw]),
                 _split_cumsum(triu_ref[...], la[c * G_CHUNK:(c + 1) * G_CHUNK, kw:2 * kw]))
                for c in range(half_chunks)]
        v_half = jnp.concatenate([_dot(xb, w_ref[:, 2 * kw + c * kw:2 * kw + (c + 1) * kw]).astype(BF16)
                                  for c in range(2)], axis=1)
        v_ref[rs, :] = v_half
        v_new.append(v_half)
        for cl in range(half_chunks):
            c = hf * half_chunks + cl
            r0 = c * G_CHUNK
            qc = q[cl * G_CHUNK:(cl + 1) * G_CHUNK]
            kc = k[cl * G_CHUNK:(cl + 1) * G_CHUNK]
            b, s = cums[cl]
            b_mid = b[G_MID - 1:G_MID]
            b_end = b[G_CHUNK - 1:G_CHUNK]
            pf_ref[r0:r0 + G_CHUNK, 0:kw] = (qc * jnp.exp2(b - b_mid)).astype(BF16)
            pf_ref[r0:r0 + G_CHUNK, kw:2 * kw] = (kc * jnp.exp2(b_mid - b)).astype(BF16)
            pf_ref[r0:r0 + G_CHUNK, 2 * kw:3 * kw] = (kc * jnp.exp2(b_end - b)).astype(BF16)
            s_mid = s[G_MID:G_MID + 1]
            s_end = s[0:1]
            pb_new[c] = jnp.concatenate([(qc * jnp.exp2(s - s_mid)).astype(BF16),
                                         (kc * jnp.exp2(s_mid - s)).astype(BF16),
                                         (kc * jnp.exp2(s_end - s)).astype(BF16)], axis=1)
            ed_new[c] = jnp.concatenate([jnp.exp2(b_mid), jnp.exp2(b_end), jnp.exp2(s_mid), jnp.exp2(s_end)], axis=0)
            ed_ref[c] = ed_new[c]
        for c in range(2):
            c0 = 2 * kw + G_VAL_WIDTH + c * kw
            g_ref[rs, c * kw:(c + 1) * kw] = _silu(_dot(xb, w_ref[:, c0:c0 + kw])).astype(BF16)
        scan_piece()
    while pending_chunks:
        scan_piece()
    scan.finish()
    for c, per_head in outs.items():
        for h, o in enumerate(per_head):
            ob_ref[c * G_CHUNK:(c + 1) * G_CHUNK, h * G_VAL_DIM:(h + 1) * G_VAL_DIM] = o.astype(BF16)
    for c in range(SCAN_CHUNKS):
        pb_s[c * G_CHUNK:(c + 1) * G_CHUNK, :] = pb_new[c]
        ed_s[c] = ed_new[c]
    v_s[...] = jnp.concatenate(v_new, axis=0)


def _l1_bwd(x2d, lr, w_bf16, wg, bg, tril, triu, blocks_per_seq):
    m = x2d.shape[0]
    n_blk = m // SCAN_ROWS
    n_main = w_bf16.shape[1]
    const2 = lambda t: (0, 0)
    proj = lambda t: (jnp.maximum(n_blk - 1 - t, 0), 0)
    proj3 = lambda t: (jnp.maximum(n_blk - 1 - t, 0), 0, 0)
    scanned = lambda t: (jnp.minimum(n_blk - t, n_blk - 1), 0)
    return pl.pallas_call(
        functools.partial(_l1_bwd_kernel, n_blk=n_blk, blocks_per_seq=blocks_per_seq),
        grid=(n_blk + 1,),
        in_specs=[pl.BlockSpec((SCAN_ROWS, D_MODEL), proj),
                  pl.BlockSpec((SCAN_ROWS, 128), proj),
                  pl.BlockSpec((D_MODEL, n_main), const2),
                  pl.BlockSpec((128, 2 * G_KEY_WIDTH), const2),
                  pl.BlockSpec((1, 2 * G_KEY_WIDTH), const2),
                  pl.BlockSpec((G_CHUNK, 2 * G_CHUNK), const2),
                  pl.BlockSpec((G_CHUNK, 2 * G_CHUNK), const2)],
        out_specs=[pl.BlockSpec((SCAN_ROWS, 3 * G_KEY_WIDTH), proj),
                   pl.BlockSpec((SCAN_ROWS, G_VAL_WIDTH), proj),
                   pl.BlockSpec((SCAN_ROWS, G_VAL_WIDTH), proj),
                   pl.BlockSpec((SCAN_CHUNKS, 4, G_KEY_WIDTH), proj3),
                   pl.BlockSpec((SCAN_ROWS, G_VAL_WIDTH), scanned)],
        out_shape=[jax.ShapeDtypeStruct((m, 3 * G_KEY_WIDTH), BF16),
                   jax.ShapeDtypeStruct((m, G_VAL_WIDTH), BF16),
                   jax.ShapeDtypeStruct((m, G_VAL_WIDTH), BF16),
                   jax.ShapeDtypeStruct((m // G_CHUNK, 4, G_KEY_WIDTH), F32),
                   jax.ShapeDtypeStruct((m, G_VAL_WIDTH), BF16)],
        scratch_shapes=[pltpu.VMEM((SCAN_ROWS, 3 * G_KEY_WIDTH), BF16),
                        pltpu.VMEM((SCAN_ROWS, G_VAL_WIDTH), BF16),
                        pltpu.VMEM((SCAN_CHUNKS, 4, G_KEY_WIDTH), F32),
                        pltpu.VMEM((G_HEADS, G_KEY_DIM, G_VAL_DIM), F32)],
        compiler_params=pltpu.CompilerParams(dimension_semantics=("arbitrary",),
                                             vmem_limit_bytes=VMEM_LIMIT_BYTES),
        name="l1_bwd",
    )(x2d, lr, w_bf16, wg, bg, tril, triu)


def _gla_fwd_kernel(p_ref, v_ref, ed_ref, ob_ref, g_ref, x_ref, hn_ref, wo_ref, lng_ref, lnb_ref,
                    o_ref, state_ref):
    @pl.when(pl.program_id(1) == 0)
    def _():
        state_ref[...] = jnp.zeros_like(state_ref)

    scan = _GlaScan(p_ref, v_ref, ed_ref, state_ref, backward=False)
    scan.attention()
    outs = {c: scan.chunk(c) for c in scan.order}
    scan.finish()
    acts = []
    for c in range(SCAN_CHUNKS):
        rsl = slice(c * G_CHUNK, (c + 1) * G_CHUNK)
        segs = []
        for h in range(G_HEADS):
            seg = outs[c][h] + ob_ref[rsl, h * G_VAL_DIM:(h + 1) * G_VAL_DIM].astype(F32)
            ms = jnp.mean(seg * seg, axis=-1, keepdims=True)
            segs.append(seg * lax.rsqrt(ms + RMS_EPS))
        on = jnp.concatenate(segs, axis=1) * hn_ref[...]
        acts.append((on * g_ref[rsl, :].astype(F32)).astype(BF16))
    half = SCAN_CHUNKS // 2
    for hf in range(2):
        rs = slice(hf * half * G_CHUNK, (hf + 1) * half * G_CHUNK)
        y = DN_ALPHA * x_ref[rs, :] + _dot(jnp.concatenate(acts[hf * half:(hf + 1) * half], axis=0), wo_ref[...])
        o_ref[rs, :] = _layer_norm(y, lng_ref[...], lnb_ref[...])


def _gla_fwd(p, v, ed, o_b, gate, x2d, head_norm, wo_bf16, ln_g, ln_b, n_seq, n_step):
    m = x2d.shape[0]
    row = lambda b, j: (b * n_step + j, 0)
    row3 = lambda b, j: (b * n_step + j, 0, 0)
    const2 = lambda b, j: (0, 0)
    return pl.pallas_call(
        _gla_fwd_kernel,
        grid=(n_seq, n_step),
        in_specs=[pl.BlockSpec((SCAN_ROWS, 3 * G_KEY_WIDTH), row),
                  pl.BlockSpec((SCAN_ROWS, G_VAL_WIDTH), row),
                  pl.BlockSpec((SCAN_CHUNKS, 4, G_KEY_WIDTH), row3),
                  pl.BlockSpec((SCAN_ROWS, G_VAL_WIDTH), row),
                  pl.BlockSpec((SCAN_ROWS, G_VAL_WIDTH), row),
                  pl.BlockSpec((SCAN_ROWS, D_MODEL), row),
                  pl.BlockSpec((1, G_VAL_WIDTH), const2),
                  pl.BlockSpec((G_VAL_WIDTH, D_MODEL), const2),
                  pl.BlockSpec((1, D_MODEL), const2),
                  pl.BlockSpec((1, D_MODEL), const2)],
        out_specs=pl.BlockSpec((SCAN_ROWS, D_MODEL), row),
        out_shape=jax.ShapeDtypeStruct((m, D_MODEL), F32),
        scratch_shapes=[pltpu.VMEM((G_HEADS, G_KEY_DIM, G_VAL_DIM), F32)],
        compiler_params=pltpu.CompilerParams(dimension_semantics=("arbitrary", "arbitrary"),
                                             vmem_limit_bytes=VMEM_LIMIT_BYTES),
        name="gla_fwd",
    )(p, v, ed, o_b, gate, x2d, head_norm, wo_bf16, ln_g, ln_b)


def _trunk(x, l0, l1):
    n_seq, seq, _ = x.shape
    assert seq % L0_ROWS == 0 and seq % SCAN_ROWS == 0 and SCAN_CHUNKS % 2 == 0
    x2d = x.reshape(n_seq * seq, D_MODEL)
    x1, lr = _l0_layer(x2d, l0["sink"], l0["w_in"], l0["bias"], l0["w_out"], l0["ln_g"], l0["ln_b"],
                       l1["w_lr"], seq // L0_ROWS)
    p_f, v, gate1, ed, o_b = _l1_bwd(x1, lr, l1["w_in"], l1["w_gate"], l1["b_gate"], l1["tril"], l1["triu"],
                                     seq // SCAN_ROWS)
    y = _gla_fwd(p_f, v, ed, o_b, gate1, x1, l1["head_norm"], l1["w_out"], l1["ln_g"], l1["ln_b"],
                 n_seq, seq // SCAN_ROWS)
    return y.reshape(n_seq, seq, D_MODEL)


def kernel(x_prompt, x_sample, l0_w_in, l0_sink, l0_w_out, l0_ln_g, l0_ln_b,
           l1_w_in, l1_w_gate_f, l1_b_gate_f, l1_w_gate_b, l1_b_gate_b, l1_head_norm,
           l1_w_out, l1_ln_g, l1_ln_b):
    row = lambda t: t.reshape(1, -1).astype(F32)
    l0 = dict(w_in=l0_w_in.astype(BF16), sink=l0_sink.astype(F32), bias=_attn_bias_table(),
              w_out=l0_w_out.astype(BF16), ln_g=row(l0_ln_g), ln_b=row(l0_ln_b))
    n_main = 2 * G_KEY_WIDTH + 2 * G_VAL_WIDTH
    w_lr = jnp.zeros((D_MODEL, 128), BF16).at[:, :2 * G_RANK].set(l1_w_in[:, n_main:].astype(BF16))
    w_gate = jnp.zeros((128, 2 * G_KEY_WIDTH), BF16)
    w_gate = w_gate.at[:G_RANK, :G_KEY_WIDTH].set(l1_w_gate_f.astype(BF16))
    w_gate = w_gate.at[G_RANK:2 * G_RANK, G_KEY_WIDTH:].set(l1_w_gate_b.astype(BF16))
    b_gate = jnp.concatenate([l1_b_gate_f, l1_b_gate_b]).reshape(1, -1).astype(F32)
    tri = np.tril(np.ones((G_CHUNK, G_CHUNK), np.float32))
    l1 = dict(w_in=l1_w_in[:, :n_main].astype(BF16), w_lr=w_lr, w_gate=w_gate, b_gate=b_gate,
              tril=jnp.asarray(np.concatenate([tri, tri], axis=1), BF16),
              triu=jnp.asarray(np.concatenate([tri.T, tri.T], axis=1), BF16),
              head_norm=row(l1_head_norm), w_out=l1_w_out.astype(BF16),
              ln_g=row(l1_ln_g), ln_b=row(l1_ln_b))
    return (_trunk(x_prompt, l0, l1), _trunk(x_sample, l0, l1))
```
